```python
import math
import jax
import jax.numpy as jnp
from jax import lax
import numpy as np

D_MODEL = 1024
BATCH = 2
SEQ = 8192
DEPTH = 2
DEC_BATCH = 128
DEC_SEQ = 8
PAST_LEN = 2048
PAGE_SIZE = 128

N_EVEN = (DEPTH + 1) // 2
N_ODD = DEPTH // 2
EPS = 1e-6

SSM_EXPAND = 2
SSM_D_INNER = SSM_EXPAND * D_MODEL
SSM_HEAD_DIM = 64
SSM_HEADS = SSM_D_INNER // SSM_HEAD_DIM
SSM_GROUPS = 2
SSM_D_STATE = 128
SSM_CONV = 4
SSM_CHUNK = 128
SSM_CONV_DIM = SSM_D_INNER + 2 * SSM_GROUPS * SSM_D_STATE

POOL_WINDOWS = (2, 4, 8, 16)
POOL_DIM = D_MODEL
POOL_GROUP = POOL_DIM // len(POOL_WINDOWS)
POOL_HIST = max(POOL_WINDOWS) - 1

HYB_IN = SSM_D_INNER + SSM_CONV_DIM + SSM_HEADS + POOL_DIM
HYB_MIX = SSM_D_INNER + POOL_DIM

ATT_HEADS = 8
ATT_DH = 64
ATT_DV = 2 * ATT_DH
ATT_WIDTH = ATT_HEADS * ATT_DV
ATT_QBLOCK = 128
REL_BUCKETS = 32
REL_MAX_DIST = 128

PEER_HEADS = 8
PEER_NKEYS = 128
PEER_EXPERTS = PEER_NKEYS * PEER_NKEYS
PEER_DKEY = 256
PEER_DHALF = PEER_DKEY // 2
PEER_TOPK = 16
PEER_BLOCK = 256

kernel_name = 'hybrid_ssd_pool_diffattn_peer_step'


def rms_norm(x, g):
    xf = x.astype(jnp.float32)
    y = xf * lax.rsqrt(jnp.mean(xf * xf, axis=-1, keepdims=True) + EPS)
    return (y * g.astype(jnp.float32)).astype(x.dtype)


def causal_dwconv(u, prev, w, b):
    l = u.shape[1]
    full = jnp.concatenate([prev.astype(u.dtype), u], axis=1)
    y = b.astype(u.dtype)
    for tap in range(SSM_CONV):
        y = y + full[:, tap:tap + l] * w[tap].astype(u.dtype)
    return y, full[:, -(SSM_CONV - 1):]


def ssd_scan(x, dt, a_neg, b_in, c_in, h0):
    f32 = jnp.float32
    n, l, nh, hp = x.shape
    g, ds = b_in.shape[2], b_in.shape[3]
    hk = nh // g
    q = SSM_CHUNK if l % SSM_CHUNK == 0 else l
    nc = l // q
    xc = x.astype(f32).reshape(n, nc, q, g, hk, hp)
    dtc = dt.reshape(n, nc, q, g, hk)
    bc = b_in.astype(f32).reshape(n, nc, q, g, ds)
    cc = c_in.astype(f32).reshape(n, nc, q, g, ds)
    xdt = xc * dtc[..., None]
    a_cs = jnp.cumsum(dtc * a_neg.reshape(g, hk), axis=2)
    seg = a_cs[:, :, :, None] - a_cs[:, :, None, :]
    causal = jnp.tril(jnp.ones((q, q), bool))[None, None, :, :, None, None]
    decay = jnp.exp(jnp.where(causal, seg, -jnp.inf))
    cb = jnp.einsum('bcigs,bcjgs->bcijg', cc, bc)
    y_diag = jnp.einsum('bcijg,bcijgk,bcjgkp->bcigkp', cb, decay, xdt)
    decay_end = jnp.exp(a_cs[:, :, -1:] - a_cs)
    states = jnp.einsum('bcjgs,bcjgk,bcjgkp->bcgkps', bc, decay_end, xdt)
    chunk_decay = jnp.exp(a_cs[:, :, -1])
    h_init = h0.astype(f32).reshape(n, g, hk, hp, ds)

    def step(h, inp):
        st, dc = inp
        return h * dc[..., None, None] + st, h

    h_last, h_in = lax.scan(step, h_init, (jnp.moveaxis(states, 1, 0), jnp.moveaxis(chunk_decay, 1, 0)))
    h_in = jnp.moveaxis(h_in, 0, 1)
    y_off = jnp.einsum('bcigs,bcigk,bcgkps->bcigkp', cc, jnp.exp(a_cs), h_in)
    y = (y_diag + y_off).reshape(n, l, nh, hp)
    return y, h_last.reshape(n, nh, hp, ds)


def pool_mixer(u, prev, pos0, w_grp, scale):
    f32 = jnp.float32
    n, l, _ = u.shape
    full = jnp.concatenate([prev.astype(u.dtype), u], axis=1)
    csum = jnp.cumsum(full.astype(f32), axis=1)
    csum = jnp.concatenate([jnp.zeros((n, 1, POOL_DIM), f32), csum], axis=1)
    end = csum[:, POOL_HIST + 1:]
    uf = u.astype(f32)
    parts = []
    for gi, w in enumerate(POOL_WINDOWS):
        sl = slice(gi * POOL_GROUP, (gi + 1) * POOL_GROUP)
        start = csum[:, POOL_HIST + 1 - w:POOL_HIST + 1 - w + l, sl]
        cnt = jnp.minimum(w, pos0 + jnp.arange(l) + 1).astype(f32)[None, :, None]
        parts.append((end[..., sl] - start) / cnt - uf[..., sl])
    pooled = jnp.stack(parts, axis=2)
    y = jnp.einsum('blgc,gcd->blgd', pooled, w_grp.astype(f32)).reshape(n, l, POOL_DIM)
    y = y * scale.astype(f32)
    return y.astype(u.dtype), full[:, -POOL_HIST:]


def hybrid_mixer(h, p, j, conv_prev, ssm_prev, pool_prev, pos0):
    f32 = jnp.float32
    n, l, _ = h.shape
    proj = h @ p['hyb_w_in'][j]
    z, xbc, dt, u = jnp.split(proj, [SSM_D_INNER, SSM_D_INNER + SSM_CONV_DIM,
                                     SSM_D_INNER + SSM_CONV_DIM + SSM_HEADS], axis=-1)
    xbc, conv_new = causal_dwconv(xbc, conv_prev, p['ssm_conv_w'][j], p['ssm_conv_b'][j])
    xbc = jax.nn.silu(xbc)
    xs, bm, cm = jnp.split(xbc, [SSM_D_INNER, SSM_D_INNER + SSM_GROUPS * SSM_D_STATE], axis=-1)
    xs = xs.reshape(n, l, SSM_HEADS, SSM_HEAD_DIM)
    bm = bm.reshape(n, l, SSM_GROUPS, SSM_D_STATE)
    cm = cm.reshape(n, l, SSM_GROUPS, SSM_D_STATE)
    dt = jax.nn.softplus(dt.astype(f32) + p['ssm_dt_bias'][j].astype(f32))
    a_neg = -jnp.exp(p['ssm_a_log'][j].astype(f32))
    y, ssm_new = ssd_scan(xs, dt, a_neg, bm, cm, ssm_prev)
    y = y + p['ssm_d'][j].astype(f32)[:, None] * xs.astype(f32)
    gsz = SSM_D_INNER // SSM_GROUPS
    y = y.reshape(n, l, SSM_GROUPS, gsz) * jax.nn.silu(z.astype(f32)).reshape(n, l, SSM_GROUPS, gsz)
    y = rms_norm(y, p['ssm_norm_g'][j].reshape(SSM_GROUPS, gsz)).reshape(n, l, SSM_D_INNER)
    pool_out, pool_new = pool_mixer(u, pool_prev, pos0, p['pool_w'][j], p['pool_scale'][j])
    mixed = jnp.concatenate([y.astype(h.dtype), pool_out.astype(h.dtype)], axis=-1)
    return mixed @ p['hyb_w_out'][j], conv_new, ssm_new, pool_new


def t5_bucket(rel):
    dist = jnp.maximum(rel, 0)
    max_exact = REL_BUCKETS // 2
    df = jnp.maximum(dist, 1).astype(jnp.float32)
    large = max_exact + (jnp.log(df / max_exact) / math.log(REL_MAX_DIST / max_exact)
                         * (REL_BUCKETS - max_exact)).astype(jnp.int32)
    large = jnp.minimum(large, REL_BUCKETS - 1)
    return jnp.where(dist < max_exact, dist, large)


def diff_attn_block(qb, qpos, k, v, kpos, rel_bias, lam):
    s = jnp.einsum('bqhmd,bkhmd->bhmqk', qb, k, preferred_element_type=jnp.float32)
    bias = rel_bias[t5_bucket(qpos[:, None] - kpos[None, :])]
    s = s + jnp.transpose(bias, (2, 0, 1)).astype(jnp.float32)[None, :, None]
    s = jnp.where(kpos[None, :] <= qpos[:, None], s, -jnp.inf)
    pr = jax.nn.softmax(s, axis=-1)
    attn = pr[:, :, 0] - lam * pr[:, :, 1]
    return jnp.einsum('bhqk,bkhd->bqhd', attn.astype(v.dtype), v)


def diff_attention(q, k, v, qpos, kpos, rel_bias, lam):
    n, l = q.shape[:2]
    if l > ATT_QBLOCK and l % ATT_QBLOCK == 0:
        nb = l // ATT_QBLOCK
        qb = jnp.moveaxis(q.reshape(n, nb, ATT_QBLOCK, ATT_HEADS, 2, ATT_DH), 1, 0)
        pb = qpos.reshape(nb, ATT_QBLOCK)
        o = lax.map(lambda a: diff_attn_block(a[0], a[1], k, v, kpos, rel_bias, lam), (qb, pb))
        return jnp.moveaxis(o, 0, 1).reshape(n, l, ATT_HEADS, ATT_DV)
    return diff_attn_block(q, qpos, k, v, kpos, rel_bias, lam)


def diff_attn_mixer(h, p, j, layer_idx, past_k, past_v, pos0):
    f32 = jnp.float32
    n, l, _ = h.shape
    q, k, v = jnp.split(h @ p['att_w_in'][j], 3, axis=-1)
    q = rms_norm(q.reshape(n, l, ATT_HEADS, 2, ATT_DH), p['att_q_g'][j]) * (ATT_DH ** -0.5)
    k = rms_norm(k.reshape(n, l, ATT_HEADS, 2, ATT_DH), p['att_k_g'][j])
    v = v.reshape(n, l, ATT_HEADS, ATT_DV)
    if past_k is None:
        k_all, v_all = k, v
    else:
        k_all = jnp.concatenate([past_k.astype(k.dtype), k], axis=1)
        v_all = jnp.concatenate([past_v.astype(v.dtype), v], axis=1)
    lam_init = 0.8 - 0.6 * math.exp(-0.3 * layer_idx)
    lam = (jnp.exp(jnp.sum(p['att_lambda_q1'][j].astype(f32) * p['att_lambda_k1'][j].astype(f32)))
           - jnp.exp(jnp.sum(p['att_lambda_q2'][j].astype(f32) * p['att_lambda_k2'][j].astype(f32)))
           + lam_init)
    q_pos = pos0 + jnp.arange(l)
    k_pos = jnp.arange(k_all.shape[1])
    o = diff_attention(q, k_all, v_all, q_pos, k_pos, p['rel_bias'], lam)
    o = rms_norm(o, p['att_subln_g'][j]) * (1.0 - lam_init)
    out = o.reshape(n, l, ATT_WIDTH) @ p['att_w_out'][j]
    return out, k.reshape(n, l, ATT_HEADS, 2 * ATT_DH), v


def peer_ffn(h, wq, subkeys, u_tab, v_tab):
    f32 = jnp.float32
    n, l, d = h.shape
    nt = n * l
    nblk = -(-nt // PEER_BLOCK)
    t = jnp.pad(h.reshape(nt, d), ((0, nblk * PEER_BLOCK - nt), (0, 0))).reshape(nblk, PEER_BLOCK, d)

    def block(tb):
        q = (tb @ wq).astype(f32).reshape(PEER_BLOCK, PEER_HEADS, 2, PEER_DHALF)
        s = jnp.einsum('thmc,mkc->thmk', q, subkeys.astype(f32))
        s1, i1 = lax.top_k(s[:, :, 0], PEER_TOPK)
        s2, i2 = lax.top_k(s[:, :, 1], PEER_TOPK)
        cand_s = (s1[..., :, None] + s2[..., None, :]).reshape(PEER_BLOCK, PEER_HEADS, PEER_TOPK * PEER_TOPK)
        cand_i = (i1[..., :, None] * PEER_NKEYS + i2[..., None, :]).reshape(PEER_BLOCK, PEER_HEADS, PEER_TOPK * PEER_TOPK)
        top_s, sel = lax.top_k(cand_s, PEER_TOPK)
        idx = jnp.take_along_axis(cand_i, sel, axis=-1)
        gate = jax.nn.softmax(top_s, axis=-1)
        act = jax.nn.gelu(jnp.einsum('thkd,td->thk', u_tab[idx], tb).astype(f32))
        return jnp.einsum('thk,thkd->td', (gate * act).astype(tb.dtype), v_tab[idx])

    out = lax.map(block, t)
    return out.reshape(nblk * PEER_BLOCK, d)[:nt].reshape(n, l, d)


def trunk(x, c, pos0, conv_prev, ssm_prev, pool_prev, paged_kv, p):
    conv_new, ssm_new, pool_new, k_new, v_new = [], [], [], [], []
    n = x.shape[0]
    c_act = jax.nn.silu(c)
    for i in range(DEPTH):
        j = i // 2
        mod = (c_act @ p['ada_w'][i] + p['ada_b'][i])[:, None, :]
        sh1, sc1, g1, sh2, sc2, g2 = jnp.split(mod, 6, axis=-1)
        h = rms_norm(x, p['norm_mix_g'][i]) * (1 + sc1) + sh1
        if i % 2 == 0:
            out, cv, st, pl = hybrid_mixer(h, p, j, conv_prev[j], ssm_prev[j], pool_prev[j], pos0)
            conv_new.append(cv.astype(x.dtype))
            ssm_new.append(st.astype(x.dtype))
            pool_new.append(pl.astype(x.dtype))
        else:
            if paged_kv is None:
                past_k = None
                past_v = None
            else:
                cache_k, cache_v, page_table = paged_kv
                past_k = cache_k[j, page_table].reshape(n, -1, ATT_HEADS, 2, ATT_DH)
                past_v = cache_v[j, page_table].reshape(n, -1, ATT_HEADS, ATT_DV)
            out, kk, vv = diff_attn_mixer(h, p, j, i, past_k, past_v, pos0)
            k_new.append(kk)
            v_new.append(vv)
        x = x + g1 * out
        h = rms_norm(x, p['norm_ffn_g'][i]) * (1 + sc2) + sh2
        x = x + g2 * peer_ffn(h, p['peer_wq'][i], p['peer_subkeys'][i], p['peer_u'][i], p['peer_v'][i])
    return (x, jnp.stack(conv_new), jnp.stack(ssm_new), jnp.stack(pool_new),
            jnp.stack(k_new), jnp.stack(v_new))


def setup_inputs(seed: int = 0) -> dict:
    key = jax.random.key(seed)
    ks = iter(jax.random.split(key, 48))
    f32 = jnp.float32

    def nrm(shape, scale):
        return jax.random.normal(next(ks), shape, f32) * scale

    n_pages = PAST_LEN // PAGE_SIZE
    n_used = DEC_BATCH * n_pages
    n_phys = n_used + n_used // 4
    d = D_MODEL
    x_prompt = nrm((BATCH, SEQ, d), 1.0)
    x_sample = nrm((DEC_BATCH, DEC_SEQ, d), 1.0)
    state_conv = nrm((N_EVEN, DEC_BATCH, SSM_CONV - 1, SSM_CONV_DIM), 1.0)
    state_ssm = nrm((N_EVEN, DEC_BATCH, SSM_HEADS, SSM_HEAD_DIM, SSM_D_STATE), 0.1)
    state_pool = nrm((N_EVEN, DEC_BATCH, POOL_HIST, POOL_DIM), 1.0)
    cache_k = nrm((N_ODD, n_phys, PAGE_SIZE, ATT_HEADS, 2 * ATT_DH), 1.0)
    cache_v = nrm((N_ODD, n_phys, PAGE_SIZE, ATT_HEADS, ATT_DV), 1.0)
    page_table = jax.random.permutation(next(ks), n_phys)[:n_used].reshape(DEC_BATCH, n_pages).astype(jnp.int32)
    c_prompt = nrm((BATCH, d), 1.0)
    c_sample = nrm((DEC_BATCH, d), 1.0)
    dt0 = jnp.exp(jax.random.uniform(next(ks), (N_EVEN, SSM_HEADS), f32, math.log(1e-3), math.log(1e-1)))
    ssm_dt_bias = dt0 + jnp.log(-jnp.expm1(-dt0))
    ssm_a_log = jnp.log(jax.random.uniform(next(ks), (N_EVEN, SSM_HEADS), f32, 1.0, 16.0))
    return {
        'x_prompt': x_prompt,
        'x_sample': x_sample,
        'state_conv': state_conv,
        'state_ssm': state_ssm,
        'state_pool': state_pool,
        'cache_k': cache_k,
        'cache_v': cache_v,
        'page_table': page_table,
        'c_prompt': c_prompt,
        'c_sample': c_sample,
        'rel_bias': nrm((REL_BUCKETS, ATT_HEADS), 0.5),
        'norm_mix_g': 1.0 + nrm((DEPTH, d), 0.1),
        'norm_ffn_g': 1.0 + nrm((DEPTH, d), 0.1),
        'ada_w': nrm((DEPTH, d, 6 * d), 0.5 * d ** -0.5),
        'ada_b': nrm((DEPTH, 6 * d), 0.02),
        'hyb_w_in': nrm((N_EVEN, d, HYB_IN), d ** -0.5),
        'ssm_conv_w': nrm((N_EVEN, SSM_CONV, SSM_CONV_DIM), SSM_CONV ** -0.5),
        'ssm_conv_b': nrm((N_EVEN, SSM_CONV_DIM), 0.02),
        'ssm_dt_bias': ssm_dt_bias,
        'ssm_a_log': ssm_a_log,
        'ssm_d': 1.0 + nrm((N_EVEN, SSM_HEADS), 0.1),
        'ssm_norm_g': 1.0 + nrm((N_EVEN, SSM_D_INNER), 0.1),
        'pool_w': nrm((N_EVEN, len(POOL_WINDOWS), POOL_GROUP, POOL_GROUP), POOL_GROUP ** -0.5),
        'pool_scale': 1.0 + nrm((N_EVEN, POOL_DIM), 0.1),
        'hyb_w_out': nrm((N_EVEN, HYB_MIX, d), HYB_MIX ** -0.5),
        'att_w_in': nrm((N_ODD, d, 3 * ATT_WIDTH), d ** -0.5),
        'att_q_g': 1.0 + nrm((N_ODD, ATT_DH), 0.1),
        'att_k_g': 1.0 + nrm((N_ODD, ATT_DH), 0.1),
        'att_lambda_q1': nrm((N_ODD, ATT_DH), 0.1),
        'att_lambda_k1': nrm((N_ODD, ATT_DH), 0.1),
        'att_lambda_q2': nrm((N_ODD, ATT_DH), 0.1),
        'att_lambda_k2': nrm((N_ODD, ATT_DH), 0.1),
        'att_subln_g': 1.0 + nrm((N_ODD, ATT_DV), 0.1),
        'att_w_out': nrm((N_ODD, ATT_WIDTH, d), ATT_WIDTH ** -0.5),
        'peer_wq': nrm((DEPTH, d, PEER_HEADS * PEER_DKEY), d ** -0.5),
        'peer_subkeys': nrm((DEPTH, 2, PEER_NKEYS, PEER_DHALF), PEER_DHALF ** -0.5),
        'peer_u': nrm((DEPTH, PEER_EXPERTS, d), d ** -0.5),
        'peer_v': nrm((DEPTH, PEER_EXPERTS, d), 1.0),
    }


def reference(x_prompt, x_sample, state_conv, state_ssm, state_pool, cache_k, cache_v, page_table,
              c_prompt, c_sample, rel_bias, norm_mix_g, norm_ffn_g, ada_w, ada_b,
              hyb_w_in, ssm_conv_w, ssm_conv_b, ssm_dt_bias, ssm_a_log, ssm_d, ssm_norm_g,
              pool_w, pool_scale, hyb_w_out, att_w_in, att_q_g, att_k_g,
              att_lambda_q1, att_lambda_k1, att_lambda_q2, att_lambda_k2, att_subln_g, att_w_out,
              peer_wq, peer_subkeys, peer_u, peer_v):
    p = {
        'rel_bias': rel_bias, 'norm_mix_g': norm_mix_g, 'norm_ffn_g': norm_ffn_g,
        'ada_w': ada_w, 'ada_b': ada_b, 'hyb_w_in': hyb_w_in,
        'ssm_conv_w': ssm_conv_w, 'ssm_conv_b': ssm_conv_b, 'ssm_dt_bias': ssm_dt_bias,
        'ssm_a_log': ssm_a_log, 'ssm_d': ssm_d, 'ssm_norm_g': ssm_norm_g,
        'pool_w': pool_w, 'pool_scale': pool_scale, 'hyb_w_out': hyb_w_out,
        'att_w_in': att_w_in, 'att_q_g': att_q_g, 'att_k_g': att_k_g,
        'att_lambda_q1': att_lambda_q1, 'att_lambda_k1': att_lambda_k1,
        'att_lambda_q2': att_lambda_q2, 'att_lambda_k2': att_lambda_k2,
        'att_subln_g': att_subln_g, 'att_w_out': att_w_out,
        'peer_wq': peer_wq, 'peer_subkeys': peer_subkeys, 'peer_u': peer_u, 'peer_v': peer_v,
    }
    nb = x_prompt.shape[0]
    conv0 = jnp.zeros((N_EVEN, nb, SSM_CONV - 1, SSM_CONV_DIM), x_prompt.dtype)
    ssm0 = jnp.zeros((N_EVEN, nb, SSM_HEADS, SSM_HEAD_DIM, SSM_D_STATE), jnp.float32)
    pool0 = jnp.zeros((N_EVEN, nb, POOL_HIST, POOL_DIM), x_prompt.dtype)
    y_prompt, conv_p, ssm_p, pool_p, k_p, v_p = trunk(
        x_prompt, c_prompt, 0, conv0, ssm0, pool0, None, p)
    past_len = page_table.shape[1] * cache_k.shape[2]
    y_sample, conv_s, ssm_s, pool_s, k_s, v_s = trunk(
        x_sample, c_sample, past_len, state_conv, state_ssm, state_pool,
        (cache_k, cache_v, page_table), p)
    return (y_prompt, y_sample, conv_p, ssm_p, pool_p, k_p, v_p, conv_s, ssm_s, pool_s, k_s, v_s)
```

```python
import functools
import math

import numpy as np
import jax
import jax.numpy as jnp
from jax import lax
from jax.experimental import pallas as pl
from jax.experimental.pallas import tpu as pltpu

F32 = jnp.float32
BF16 = jnp.bfloat16

D_MODEL = 1024
EPS = 1e-6
LANES = 128
SUBLANES = 8

SSM_D_INNER = 2048
SSM_HEAD_DIM = 64
SSM_HEADS = 32
SSM_GROUPS = 2
SSM_D_STATE = 128
SSM_CONV = 4
SSM_CHUNK = 128
SSM_CONV_DIM = 2560
GROUP_W = SSM_D_INNER // SSM_GROUPS

POOL_WINDOWS = (2, 4, 8, 16)
POOL_DIM = 1024
POOL_GROUP = 256
POOL_HIST = 15
POOL_HALO = 16
CONV_HALO = 8

HYB_PROJ = SSM_D_INNER + SSM_CONV_DIM + POOL_DIM + LANES
HYB_MIX = SSM_D_INNER + POOL_DIM

ATT_HEADS = 8
ATT_DH = 64
ATT_DV = 128
ATT_WIDTH = 1024
REL_BUCKETS = 32
REL_MAX_DIST = 128
PAGE = 128

PEER_HEADS = 8
PEER_NKEYS = 128
PEER_EXPERTS = PEER_NKEYS * PEER_NKEYS
PEER_TOPK = 16
PEER_NTOP = PEER_TOPK + 1
PEER_EB = 1024
PEER_IB = PEER_EB // PEER_NKEYS

NEG_INF = float("-inf")
NT_DIMS = (((1,), (1,)), ((), ()))

VMEM_LIMIT = 56 * 1024 * 1024


def _params(sem):
    return pltpu.CompilerParams(dimension_semantics=sem, vmem_limit_bytes=VMEM_LIMIT)


def _sigmoid(x):
    return 1.0 / (1.0 + jnp.exp(-x))


def _silu(x):
    return x * _sigmoid(x)


def _split3(x):
    hi = x.astype(BF16)
    r1 = x - hi.astype(F32)
    mid = r1.astype(BF16)
    lo = (r1 - mid.astype(F32)).astype(BF16)
    return hi, mid, lo


def _norm_mod(x3, g_ref, sc_ref, sh_ref):
    ms = jnp.mean(x3 * x3, axis=-1, keepdims=True)
    y = x3 * lax.rsqrt(ms + EPS) * g_ref[...]
    return y * (1.0 + sc_ref[...]) + sh_ref[...]


def _row_blocks(n, l):
    if l >= 512:
        return 1, 512
    if l >= SUBLANES and l % SUBLANES == 0 and l < 512:
        if l == SUBLANES:
            nb = min(n, 64)
            while n % nb:
                nb //= 2
            return nb, l
        return 1, l
    raise ValueError(f"unsupported sequence length {l}")


def _ada_kernel(c_ref, w_ref, b_ref, o_ref):
    ca = _silu(c_ref[...]).astype(BF16)
    o_ref[0] = jnp.dot(ca, w_ref[0].astype(BF16), preferred_element_type=F32) + b_ref[0]


def ada_mod(c_all, ada_w, ada_b):
    r = c_all.shape[0]
    depth, d, n6 = ada_w.shape
    tn = 512
    return pl.pallas_call(
        _ada_kernel,
        grid=(depth, n6 // tn),
        in_specs=[pl.BlockSpec((r, d), lambda i, j: (0, 0)),
                  pl.BlockSpec((1, d, tn), lambda i, j: (i, 0, j)),
                  pl.BlockSpec((1, 1, tn), lambda i, j: (i, 0, j))],
        out_specs=pl.BlockSpec((1, r, tn), lambda i, j: (i, 0, j)),
        out_shape=jax.ShapeDtypeStruct((depth, r, n6), F32),
        compiler_params=_params(("arbitrary", "arbitrary")),
        name="ada_mod",
    )(c_all, ada_w, ada_b.reshape(depth, 1, n6))


def _inproj_kernel(x_ref, g_ref, sc_ref, sh_ref, w_ref, o_ref, h_scr):
    @pl.when(pl.program_id(2) == 0)
    def _():
        h = _norm_mod(x_ref[...], g_ref, sc_ref, sh_ref)
        h_scr[...] = h.reshape(h_scr.shape).astype(BF16)

    o_ref[...] = jnp.dot(h_scr[...], w_ref[...], preferred_element_type=F32)


def norm_inproj(x, g, sc, sh, w, tn):
    n, l, d = x.shape
    nb, lb = _row_blocks(n, l)
    m = nb * lb
    lk = l // lb
    nout = w.shape[1]
    return pl.pallas_call(
        _inproj_kernel,
        grid=(n // nb, lk, nout // tn),
        in_specs=[pl.BlockSpec((nb, lb, d), lambda i, k, j: (i, k, 0)),
                  pl.BlockSpec((1, d), lambda i, k, j: (0, 0)),
                  pl.BlockSpec((nb, 1, d), lambda i, k, j: (i, 0, 0)),
                  pl.BlockSpec((nb, 1, d), lambda i, k, j: (i, 0, 0)),
                  pl.BlockSpec((d, tn), lambda i, k, j: (0, j))],
        out_specs=pl.BlockSpec((m, tn), lambda i, k, j: (i * lk + k, j)),
        out_shape=jax.ShapeDtypeStruct((n * l, nout), F32),
        scratch_shapes=[pltpu.VMEM((m, d), BF16)],
        compiler_params=_params(("arbitrary", "arbitrary", "arbitrary")),
        name="norm_inproj",
    )(x, g.reshape(1, d), sc, sh, w)


def _seg_rms(y, g128, bd):
    sq = y * y
    hi = sq.astype(BF16)
    lo = (sq - hi.astype(F32)).astype(BF16)
    outs = []
    for cb in range(y.shape[1] // LANES):
        sl = slice(cb * LANES, (cb + 1) * LANES)
        ms = (jnp.dot(hi[:, sl], bd, preferred_element_type=F32)
              + jnp.dot(lo[:, sl], bd, preferred_element_type=F32))
        outs.append(y[:, sl] * lax.rsqrt(ms + EPS) * g128)
    return jnp.concatenate(outs, axis=1)


def _qkv_kernel(x_ref, g_ref, sc_ref, sh_ref, w_ref, qg_ref, kg_ref, bd_ref,
                q_o, kf_o, kb_o, vf_o, vb_o, h_scr):
    j = pl.program_id(2)

    @pl.when(j == 0)
    def _():
        h = _norm_mod(x_ref[...], g_ref, sc_ref, sh_ref)
        h_scr[...] = h.reshape(h_scr.shape).astype(BF16)

    y = jnp.dot(h_scr[...], w_ref[...], preferred_element_type=F32)

    @pl.when(j == 0)
    def _():
        q_o[...] = (_seg_rms(y, qg_ref[...], bd_ref[...]) * (ATT_DH ** -0.5)).astype(BF16)

    @pl.when(j == 1)
    def _():
        kn = _seg_rms(y, kg_ref[...], bd_ref[...])
        kf_o[...] = kn
        kb_o[...] = kn.astype(BF16)

    @pl.when(j == 2)
    def _():
        vf_o[...] = y
        vb_o[...] = y.astype(BF16)


def norm_qkv(x, g, sc, sh, w, q_g, k_g):
    n, l, d = x.shape
    nb, lb = _row_blocks(n, l)
    m = nb * lb
    lk = l // lb
    t = n * l
    seg = np.arange(LANES) // ATT_DH
    bd = jnp.asarray((seg[:, None] == seg[None, :]).astype(np.float32) / ATT_DH, BF16)
    row = lambda i, k, j: (i * lk + k, 0)
    const2 = lambda i, k, j: (0, 0)
    outs = pl.pallas_call(
        _qkv_kernel,
        grid=(n // nb, lk, 3),
        in_specs=[pl.BlockSpec((nb, lb, d), lambda i, k, j: (i, k, 0)),
                  pl.BlockSpec((1, d), const2),
                  pl.BlockSpec((nb, 1, d), lambda i, k, j: (i, 0, 0)),
                  pl.BlockSpec((nb, 1, d), lambda i, k, j: (i, 0, 0)),
                  pl.BlockSpec((d, ATT_WIDTH), lambda i, k, j: (0, j)),
                  pl.BlockSpec((1, LANES), const2),
                  pl.BlockSpec((1, LANES), const2),
                  pl.BlockSpec((LANES, LANES), const2)],
        out_specs=[pl.BlockSpec((m, ATT_WIDTH), row)] * 5,
        out_shape=[jax.ShapeDtypeStruct((t, ATT_WIDTH), BF16),
                   jax.ShapeDtypeStruct((t, ATT_WIDTH), F32),
                   jax.ShapeDtypeStruct((t, ATT_WIDTH), BF16),
                   jax.ShapeDtypeStruct((t, ATT_WIDTH), F32),
                   jax.ShapeDtypeStruct((t, ATT_WIDTH), BF16)],
        scratch_shapes=[pltpu.VMEM((m, d), BF16)],
        compiler_params=_params(("arbitrary", "arbitrary", "arbitrary")),
        name="norm_qkv",
    )(x, g.reshape(1, d), sc, sh, w,
      jnp.tile(q_g, 2).reshape(1, LANES), jnp.tile(k_g, 2).reshape(1, LANES), bd)
    return outs


def _outproj_kernel(a_ref, w_ref, x_ref, g1_ref, gn_ref, sc_ref, sh_ref, x1_o, hn_o):
    y = jnp.dot(a_ref[...], w_ref[...], preferred_element_type=F32)
    x1 = x_ref[...] + g1_ref[...] * y.reshape(x_ref.shape)
    x1_o[...] = x1
    hn_o[...] = _norm_mod(x1, gn_ref, sc_ref, sh_ref).reshape(hn_o.shape).astype(BF16)


def outproj_residual(a, w, x, g1, gn, sc, sh):
    n, l, d = x.shape
    nb, lb = _row_blocks(n, l)
    m = nb * lb
    lk = l // lb
    kdim = a.shape[1]
    seq = lambda i, k: (i, 0, 0)
    return pl.pallas_call(
        _outproj_kernel,
        grid=(n // nb, lk),
        in_specs=[pl.BlockSpec((m, kdim), lambda i, k: (i * lk + k, 0)),
                  pl.BlockSpec((kdim, d), lambda i, k: (0, 0)),
                  pl.BlockSpec((nb, lb, d), lambda i, k: (i, k, 0)),
                  pl.BlockSpec((nb, 1, d), seq),
                  pl.BlockSpec((1, d), lambda i, k: (0, 0)),
                  pl.BlockSpec((nb, 1, d), seq),
                  pl.BlockSpec((nb, 1, d), seq)],
        out_specs=[pl.BlockSpec((nb, lb, d), lambda i, k: (i, k, 0)),
                   pl.BlockSpec((m, d), lambda i, k: (i * lk + k, 0))],
        out_shape=[jax.ShapeDtypeStruct((n, l, d), F32),
                   jax.ShapeDtypeStruct((n * l, d), BF16)],
        compiler_params=_params(("arbitrary", "arbitrary")),
        name="outproj_residual",
    )(a, w, x, g1, gn.reshape(1, d), sc, sh)


def _ssd_pool_kernel(proj_ref, convp_ref, ssmp_ref, poolp_ref, convw_ref, convb_ref, dtb_ref, alog_ref,
                     dexp_ref, normg_ref, poolw_ref, pools_ref, tril_ref, rexp_ref,
                     mixed_o, convn_o, ssmn_o, pooln_o,
                     ext_c, ext_p, ht_scr, *, lr, pos0):
    q = SSM_CHUNK
    c = pl.program_id(1)
    nc = pl.num_programs(1)

    @pl.when(c == 0)
    def _():
        ext_c[0:CONV_HALO, :] = convp_ref[0]
        ext_p[0:POOL_HALO, :] = poolp_ref[0]
        for g in range(SSM_GROUPS):
            ht_scr[g] = ssmp_ref[0, g * GROUP_W:(g + 1) * GROUP_W, :].T

    xbc_cols = slice(SSM_D_INNER, SSM_D_INNER + SSM_CONV_DIM)
    u_cols = slice(SSM_D_INNER + SSM_CONV_DIM, SSM_D_INNER + SSM_CONV_DIM + POOL_DIM)
    dt_cols = slice(SSM_D_INNER + SSM_CONV_DIM + POOL_DIM, HYB_PROJ)

    ext_c[CONV_HALO:CONV_HALO + lr, :] = proj_ref[:, xbc_cols]
    ext_p[POOL_HALO:POOL_HALO + lr, :] = proj_ref[:, u_cols]
    if lr < q:
        ext_c[CONV_HALO + lr:CONV_HALO + q, :] = jnp.zeros((q - lr, SSM_CONV_DIM), F32)
        ext_p[POOL_HALO + lr:POOL_HALO + q, :] = jnp.zeros((q - lr, POOL_DIM), F32)

    acc = convb_ref[...] + ext_c[pl.ds(CONV_HALO - 3, q), :] * convw_ref[0:1, :]
    for tap in range(1, SSM_CONV):
        acc = acc + ext_c[pl.ds(CONV_HALO - 3 + tap, q), :] * convw_ref[tap:tap + 1, :]
    xbc = _silu(acc)
    xs = xbc[:, :SSM_D_INNER]

    row = lax.broadcasted_iota(jnp.int32, (q, LANES), 0)
    dt_raw = proj_ref[:, dt_cols] + dtb_ref[...]
    if lr < q:
        dt_raw = jnp.concatenate([dt_raw, jnp.zeros((q - lr, LANES), F32)], axis=0)
    dt = jnp.maximum(dt_raw, 0.0) + jnp.log(1.0 + jnp.exp(-jnp.abs(dt_raw)))
    dt = jnp.where(row < lr, dt, 0.0)
    a_neg = -jnp.exp(alog_ref[...])
    d_a = dt * a_neg

    tril = tril_ref[...]
    rexp = rexp_ref[...]
    a_cs = sum(jnp.dot(tril, p, preferred_element_type=F32) for p in _split3(d_a))
    acs_e = sum(jnp.dot(p, rexp, preferred_element_type=F32) for p in _split3(a_cs))
    dt_e = sum(jnp.dot(p, rexp, preferred_element_type=F32) for p in _split3(dt))
    a_cs_t = a_cs.T

    xdt = xs * dt_e
    alast_e = acs_e[lr - 1:lr, :]
    xdtw = (xdt * jnp.exp(alast_e - acs_e)).astype(BF16)
    ea_e = jnp.exp(acs_e)
    xdt_b = xdt.astype(BF16)

    ii = lax.broadcasted_iota(jnp.int32, (q, q), 0)
    jj = lax.broadcasted_iota(jnp.int32, (q, q), 1)
    causal = ii >= jj
    lane = lax.broadcasted_iota(jnp.int32, (q, LANES), 1)
    even_half = lane < SSM_HEAD_DIM
    zero_b = jnp.zeros((q, LANES), BF16)

    ys = []
    for g in range(SSM_GROUPS):
        bg = xbc[:, SSM_D_INNER + g * SSM_D_STATE:SSM_D_INNER + (g + 1) * SSM_D_STATE]
        cg = xbc[:, SSM_D_INNER + SSM_GROUPS * SSM_D_STATE + g * SSM_D_STATE:
                 SSM_D_INNER + SSM_GROUPS * SSM_D_STATE + (g + 1) * SSM_D_STATE].astype(BF16)
        bg_t = bg.T.astype(BF16)
        gcols = slice(g * GROUP_W, (g + 1) * GROUP_W)
        cb = jnp.dot(cg, bg_t, preferred_element_type=F32)
        h_in = ht_scr[g]
        y_off = jnp.dot(cg, h_in.astype(BF16), preferred_element_type=F32) * ea_e[:, gcols]
        st = jnp.dot(bg_t, xdtw[:, gcols], preferred_element_type=F32)
        ht_scr[g] = h_in * jnp.exp(alast_e[:, gcols]) + st
        pieces = []
        for pr in range(GROUP_W // LANES):
            ms = []
            for hh in range(2):
                h = g * (SSM_HEADS // SSM_GROUPS) + 2 * pr + hh
                seg = a_cs[:, h:h + 1] - a_cs_t[h:h + 1, :]
                ms.append((cb * jnp.exp(jnp.where(causal, seg, NEG_INF))).astype(BF16))
            xblk = xdt_b[:, g * GROUP_W + pr * LANES:g * GROUP_W + (pr + 1) * LANES]
            rhs = jnp.concatenate([jnp.where(even_half, xblk, zero_b),
                                   jnp.where(even_half, zero_b, xblk)], axis=0)
            pieces.append(jnp.dot(jnp.concatenate(ms, axis=1), rhs, preferred_element_type=F32))
        ys.append(jnp.concatenate(pieces, axis=1) + y_off)
    y = jnp.concatenate(ys, axis=1) + dexp_ref[...] * xs
    y = y * _silu(proj_ref[:, :SSM_D_INNER]) if lr == q else (
        y * _silu(jnp.concatenate([proj_ref[:, :SSM_D_INNER], jnp.zeros((q - lr, SSM_D_INNER), F32)], axis=0)))
    for g in range(SSM_GROUPS):
        gcols = slice(g * GROUP_W, (g + 1) * GROUP_W)
        yg = y[:, gcols]
        ms = jnp.mean(yg * yg, axis=-1, keepdims=True)
        yn = yg * lax.rsqrt(ms + EPS) * normg_ref[:, gcols]
        mixed_o[:, gcols] = yn[:lr].astype(BF16)

    pos = pos0 + c * q + lax.broadcasted_iota(jnp.int32, (q, POOL_GROUP), 0)
    for gi, w in enumerate(POOL_WINDOWS):
        cols = slice(gi * POOL_GROUP, (gi + 1) * POOL_GROUP)
        cur = ext_p[pl.ds(POOL_HALO, q), cols]
        wsum = cur
        for k in range(1, w):
            wsum = wsum + ext_p[pl.ds(POOL_HALO - k, q), cols]
        cnt = jnp.minimum(w, pos + 1).astype(F32)
        pooled = wsum / cnt - cur
        yp = jnp.dot(pooled.astype(BF16), poolw_ref[gi], preferred_element_type=F32) * pools_ref[:, cols]
        mixed_o[:, SSM_D_INNER + gi * POOL_GROUP:SSM_D_INNER + (gi + 1) * POOL_GROUP] = yp[:lr].astype(BF16)

    new_c = ext_c[lr:lr + CONV_HALO, :]
    new_p = ext_p[lr:lr + POOL_HALO, :]
    ext_c[0:CONV_HALO, :] = new_c
    ext_p[0:POOL_HALO, :] = new_p

    @pl.when(c == nc - 1)
    def _():
        convn_o[0] = new_c
        pooln_o[0] = new_p
        for g in range(SSM_GROUPS):
            ssmn_o[0, g * GROUP_W:(g + 1) * GROUP_W, :] = ht_scr[g].T


def ssd_pool(proj, n, l, conv_prev, ssm_prev, pool_prev, pos0, p):
    q = SSM_CHUNK
    if l % q == 0:
        lr, nc = q, l // q
    elif l < q and l % SUBLANES == 0:
        lr, nc = l, 1
    else:
        raise ValueError(f"unsupported sequence length {l}")
    convp = jnp.pad(conv_prev, ((0, 0), (CONV_HALO - (SSM_CONV - 1), 0), (0, 0)))
    poolp = jnp.pad(pool_prev, ((0, 0), (POOL_HALO - POOL_HIST, 0), (0, 0)))
    ssmp = ssm_prev.reshape(n, SSM_D_INNER, SSM_D_STATE)
    pad_h = LANES - SSM_HEADS
    tril = jnp.asarray(np.tril(np.ones((q, q), np.float32)), BF16)
    rexp_np = np.zeros((LANES, SSM_D_INNER), np.float32)
    rexp_np[np.arange(SSM_D_INNER) // SSM_HEAD_DIM, np.arange(SSM_D_INNER)] = 1.0
    rexp = jnp.asarray(rexp_np, BF16)
    c2 = lambda b, c: (0, 0)
    seq3 = lambda b, c: (b, 0, 0)
    kern = functools.partial(_ssd_pool_kernel, lr=lr, pos0=pos0)
    mixed, convn, ssmn, pooln = pl.pallas_call(
        kern,
        grid=(n, nc),
        in_specs=[pl.BlockSpec((lr, HYB_PROJ), lambda b, c: (b * nc + c, 0)),
                  pl.BlockSpec((1, CONV_HALO, SSM_CONV_DIM), seq3),
                  pl.BlockSpec((1, SSM_D_INNER, SSM_D_STATE), seq3),
                  pl.BlockSpec((1, POOL_HALO, POOL_DIM), seq3),
                  pl.BlockSpec((SSM_CONV, SSM_CONV_DIM), c2),
                  pl.BlockSpec((1, SSM_CONV_DIM), c2),
                  pl.BlockSpec((1, LANES), c2),
                  pl.BlockSpec((1, LANES), c2),
                  pl.BlockSpec((1, SSM_D_INNER), c2),
                  pl.BlockSpec((1, SSM_D_INNER), c2),
                  pl.BlockSpec((len(POOL_WINDOWS), POOL_GROUP, POOL_GROUP), lambda b, c: (0, 0, 0)),
                  pl.BlockSpec((1, POOL_DIM), c2),
                  pl.BlockSpec((q, q), c2),
                  pl.BlockSpec((LANES, SSM_D_INNER), c2)],
        out_specs=[pl.BlockSpec((lr, HYB_MIX), lambda b, c: (b * nc + c, 0)),
                   pl.BlockSpec((1, CONV_HALO, SSM_CONV_DIM), seq3),
                   pl.BlockSpec((1, SSM_D_INNER, SSM_D_STATE), seq3),
                   pl.BlockSpec((1, POOL_HALO, POOL_DIM), seq3)],
        out_shape=[jax.ShapeDtypeStruct((n * l, HYB_MIX), BF16),
                   jax.ShapeDtypeStruct((n, CONV_HALO, SSM_CONV_DIM), F32),
                   jax.ShapeDtypeStruct((n, SSM_D_INNER, SSM_D_STATE), F32),
                   jax.ShapeDtypeStruct((n, POOL_HALO, POOL_DIM), F32)],
        scratch_shapes=[pltpu.VMEM((CONV_HALO + q, SSM_CONV_DIM), F32),
                        pltpu.VMEM((POOL_HALO + q, POOL_DIM), F32),
                        pltpu.VMEM((SSM_GROUPS, SSM_D_STATE, GROUP_W), F32)],
        compiler_params=_params(("arbitrary", "arbitrary")),
        name="ssd_pool",
    )(proj, convp, ssmp, poolp,
      p["ssm_conv_w"], p["ssm_conv_b"].reshape(1, -1),
      jnp.pad(p["ssm_dt_bias"], (0, pad_h)).reshape(1, LANES),
      jnp.pad(p["ssm_a_log"], (0, pad_h)).reshape(1, LANES),
      jnp.repeat(p["ssm_d"], SSM_HEAD_DIM).reshape(1, SSM_D_INNER),
      p["ssm_norm_g"].reshape(1, SSM_D_INNER),
      p["pool_w"].astype(BF16), p["pool_scale"].reshape(1, POOL_DIM), tril, rexp)
    conv_new = convn[:, CONV_HALO - (SSM_CONV - 1):, :]
    ssm_new = ssmn.reshape(n, SSM_HEADS, SSM_HEAD_DIM, SSM_D_STATE)
    pool_new = pooln[:, POOL_HALO - POOL_HIST:, :]
    return mixed, conv_new, ssm_new, pool_new


def _bucket_bounds():
    max_exact = REL_BUCKETS // 2
    bounds = []
    for b in range(max_exact + 1, REL_BUCKETS):
        d = max_exact
        while True:
            val = math.log(d / max_exact) / math.log(REL_MAX_DIST / max_exact) * (REL_BUCKETS - max_exact)
            if max_exact + int(val) >= b:
                break
            d += 1
        bounds.append(d)
    return bounds


def _bias_kernel(rb_ref, o_ref, *, offsets, rows, cols):
    h = pl.program_id(0)
    max_exact = REL_BUCKETS // 2
    bounds = _bucket_bounds()
    far = rb_ref[REL_BUCKETS - 1, h]
    ii = lax.broadcasted_iota(jnp.int32, (rows, cols), 0)
    jj = lax.broadcasted_iota(jnp.int32, (rows, cols), 1)
    for oi, off in enumerate(offsets):
        dist = jnp.maximum(off + ii - jj, 0)
        large = jnp.full((rows, cols), max_exact, jnp.int32)
        for bnd in bounds:
            large = large + (dist >= bnd).astype(jnp.int32)
        bucket = jnp.where(dist < max_exact, dist, large)
        bias = jnp.zeros((rows, cols), F32)
        for b in range(REL_BUCKETS):
            bias = bias + jnp.where(bucket == b, rb_ref[b, h] - far, 0.0)
        o_ref[0, oi] = bias


def bias_tiles(rel_bias, offsets, rows, cols):
    kern = functools.partial(_bias_kernel, offsets=tuple(offsets), rows=rows, cols=cols)
    return pl.pallas_call(
        kern,
        grid=(ATT_HEADS,),
        in_specs=[pl.BlockSpec(memory_space=pltpu.SMEM)],
        out_specs=pl.BlockSpec((1, len(offsets), rows, cols), lambda h: (h, 0, 0, 0)),
        out_shape=jax.ShapeDtypeStruct((ATT_HEADS, len(offsets), rows, cols), F32),
        compiler_params=_params(("arbitrary",)),
        name="bias_tiles",
    )(rel_bias)


def _lambda(lam_ref, lam_init):
    l1 = jnp.sum(lam_ref[0:1, :] * lam_ref[1:2, :], axis=-1, keepdims=True)
    l2 = jnp.sum(lam_ref[2:3, :] * lam_ref[3:4, :], axis=-1, keepdims=True)
    return jnp.exp(l1) - jnp.exp(l2) + lam_init


def _subln(o, sg_ref, lam_init):
    ms = jnp.mean(o * o, axis=-1, keepdims=True)
    return o * lax.rsqrt(ms + EPS) * sg_ref[...] * (1.0 - lam_init)


def _attn_kernel(lam_ref, q_ref, k_ref, v_ref, bias_ref, sg_ref, o_ref, m_scr, l_scr, acc_scr, *, tq, lam_init):
    qi = pl.program_id(2)
    q = q_ref[...]
    lane = lax.broadcasted_iota(jnp.int32, (tq, LANES), 1)
    zero = jnp.zeros_like(q)
    q_halves = (jnp.where(lane < ATT_DH, q, zero), jnp.where(lane < ATT_DH, zero, q))
    m_scr[...] = jnp.full(m_scr.shape, NEG_INF, F32)
    l_scr[...] = jnp.zeros(l_scr.shape, F32)
    acc_scr[...] = jnp.zeros(acc_scr.shape, F32)
    ii = lax.broadcasted_iota(jnp.int32, (tq, tq), 0)
    jj = lax.broadcasted_iota(jnp.int32, (tq, tq), 1)

    def step(ki, bias, causal):
        off = pl.multiple_of(ki * tq, tq)
        k = k_ref[pl.ds(off, tq), :]
        v = v_ref[pl.ds(off, tq), :]
        for mi in range(2):
            s = lax.dot_general(q_halves[mi], k, NT_DIMS, preferred_element_type=F32)
            if bias is not None:
                s = s + bias
            if causal:
                s = jnp.where(ii >= jj, s, NEG_INF)
            m_old = m_scr[mi]
            m_new = jnp.maximum(m_old, jnp.max(s, axis=-1, keepdims=True))
            alpha = jnp.exp(m_old - m_new)
            pr = jnp.exp(s - m_new)
            l_scr[mi] = alpha * l_scr[mi] + jnp.sum(pr, axis=-1, keepdims=True)
            acc_scr[mi] = alpha * acc_scr[mi] + jnp.dot(pr.astype(BF16), v, preferred_element_type=F32)
            m_scr[mi] = m_new

    def far_step(ki, carry):
        step(ki, None, False)
        return carry

    lax.fori_loop(0, jnp.maximum(qi - 1, 0), far_step, 0)

    @pl.when(qi >= 1)
    def _():
        step(qi - 1, bias_ref[0, 1], False)

    step(qi, bias_ref[0, 0], True)
    lam = _lambda(lam_ref, lam_init)
    o = acc_scr[0] / l_scr[0] - lam * (acc_scr[1] / l_scr[1])
    o_ref[...] = _subln(o, sg_ref, lam_init).astype(BF16)


def diff_attention_prompt(q, k, v, n, l, bias, lam_vecs, subln_g, lam_init):
    tq = min(256, l)
    nq = l // tq
    kern = functools.partial(_attn_kernel, tq=tq, lam_init=lam_init)
    return pl.pallas_call(
        kern,
        grid=(n, ATT_HEADS, nq),
        in_specs=[pl.BlockSpec((4, ATT_DH), lambda b, h, i: (0, 0)),
                  pl.BlockSpec((tq, LANES), lambda b, h, i: (b * nq + i, h)),
                  pl.BlockSpec((l, LANES), lambda b, h, i: (b, h)),
                  pl.BlockSpec((l, LANES), lambda b, h, i: (b, h)),
                  pl.BlockSpec((1, 2, tq, tq), lambda b, h, i: (h, 0, 0, 0)),
                  pl.BlockSpec((1, ATT_DV), lambda b, h, i: (0, 0))],
        out_specs=pl.BlockSpec((tq, LANES), lambda b, h, i: (b * nq + i, h)),
        out_shape=jax.ShapeDtypeStruct((n * l, ATT_WIDTH), BF16),
        scratch_shapes=[pltpu.VMEM((2, tq, 1), F32), pltpu.VMEM((2, tq, 1), F32),
                        pltpu.VMEM((2, tq, ATT_DV), F32)],
        compiler_params=_params(("arbitrary", "arbitrary", "arbitrary")),
        name="diff_attn_prompt",
    )(lam_vecs, q, k, v, bias, subln_g.reshape(1, ATT_DV))


def _attn_decode_kernel(pt_ref, lam_ref, q_ref, kn_ref, vn_ref, bias_ref, sg_ref, *rest, pps, l, lam_init):
    k_refs = rest[:pps]
    v_refs = rest[pps:2 * pps]
    o_ref = rest[2 * pps]
    qbd_scr, m_scr, l_scr, acc_scr = rest[2 * pps + 1:]
    s_idx = pl.program_id(1)
    ns = pl.num_programs(1)
    rows = 2 * ATT_HEADS * l

    @pl.when(s_idx == 0)
    def _():
        qt = jnp.concatenate([q_ref[0]] * (2 * ATT_HEADS), axis=0)
        r = lax.broadcasted_iota(jnp.int32, (rows, ATT_WIDTH), 0)
        cidx = lax.broadcasted_iota(jnp.int32, (rows, ATT_WIDTH), 1)
        qbd_scr[...] = jnp.where(cidx // ATT_DH == r // l, qt, jnp.zeros_like(qt))
        m_scr[...] = jnp.full(m_scr.shape, NEG_INF, F32)
        l_scr[...] = jnp.zeros(l_scr.shape, F32)
        acc_scr[...] = jnp.zeros(acc_scr.shape, F32)

    def update(s, vb):
        m_old = m_scr[...]
        m_new = jnp.maximum(m_old, jnp.max(s, axis=-1, keepdims=True))
        alpha = jnp.exp(m_old - m_new)
        pr = jnp.exp(s - m_new)
        l_scr[...] = alpha * l_scr[...] + jnp.sum(pr, axis=-1, keepdims=True)
        acc_scr[...] = alpha * acc_scr[...] + jnp.dot(pr.astype(BF16), vb, preferred_element_type=F32)
        m_scr[...] = m_new

    qbd = qbd_scr[...]
    kb = jnp.concatenate([kr[0].astype(BF16) for kr in k_refs], axis=0)
    vb = jnp.concatenate([vr[0].astype(BF16) for vr in v_refs], axis=0)
    s = lax.dot_general(qbd, kb, NT_DIMS, preferred_element_type=F32)

    @pl.when(s_idx < ns - 1)
    def _():
        update(s, vb)

    @pl.when(s_idx == ns - 1)
    def _():
        sb = jnp.concatenate([s[:, :(pps - 1) * PAGE], s[:, (pps - 1) * PAGE:] + bias_ref[0]], axis=1) \
            if pps > 1 else s + bias_ref[0]
        update(sb, vb)
        pad = jnp.zeros((PAGE - l, ATT_WIDTH), F32)
        kn = jnp.concatenate([kn_ref[0], pad], axis=0).astype(BF16)
        vn = jnp.concatenate([vn_ref[0], pad], axis=0).astype(BF16)
        sn = lax.dot_general(qbd, kn, NT_DIMS, preferred_element_type=F32) + bias_ref[1]
        r = lax.broadcasted_iota(jnp.int32, (rows, PAGE), 0)
        j = lax.broadcasted_iota(jnp.int32, (rows, PAGE), 1)
        sn = jnp.where(j <= r % l, sn, NEG_INF)
        update(sn, vn)
        lam = _lambda(lam_ref, lam_init)
        outs = []
        for h in range(ATT_HEADS):
            r0 = 2 * h * l
            cols = slice(h * ATT_DV, (h + 1) * ATT_DV)
            o0 = acc_scr[r0:r0 + l, cols] / l_scr[r0:r0 + l, :]
            o1 = acc_scr[r0 + l:r0 + 2 * l, cols] / l_scr[r0 + l:r0 + 2 * l, :]
            outs.append(_subln(o0 - lam * o1, sg_ref, lam_init))
        o_ref[0] = jnp.concatenate(outs, axis=1).astype(BF16)


def diff_attention_decode(q, k_new, v_new, cache_k, cache_v, page_table, bias, lam_vecs, subln_g, lam_init):
    n, l, _ = q.shape
    n_pages = page_table.shape[1]
    pps = 4 if n_pages % 4 == 0 else (2 if n_pages % 2 == 0 else 1)
    ns = n_pages // pps
    rows = 2 * ATT_HEADS * l
    kern = functools.partial(_attn_decode_kernel, pps=pps, l=l, lam_init=lam_init)

    def page_spec(pi):
        return pl.BlockSpec((1, PAGE, ATT_WIDTH), lambda b, s, pt: (pt[b * n_pages + s * pps + pi], 0, 0))

    seq = lambda b, s, pt: (b, 0, 0)
    grid_spec = pltpu.PrefetchScalarGridSpec(
        num_scalar_prefetch=1,
        grid=(n, ns),
        in_specs=[pl.BlockSpec((4, ATT_DH), lambda b, s, pt: (0, 0)),
                  pl.BlockSpec((1, l, ATT_WIDTH), seq),
                  pl.BlockSpec((1, l, ATT_WIDTH), seq),
                  pl.BlockSpec((1, l, ATT_WIDTH), seq),
                  pl.BlockSpec((2, rows, PAGE), lambda b, s, pt: (0, 0, 0)),
                  pl.BlockSpec((1, ATT_DV), lambda b, s, pt: (0, 0))]
                 + [page_spec(pi) for pi in range(pps)] * 2,
        out_specs=pl.BlockSpec((1, l, ATT_WIDTH), seq),
        scratch_shapes=[pltpu.VMEM((rows, ATT_WIDTH), BF16), pltpu.VMEM((rows, 1), F32),
                        pltpu.VMEM((rows, 1), F32), pltpu.VMEM((rows, ATT_WIDTH), F32)])
    return pl.pallas_call(
        kern,
        grid_spec=grid_spec,
        out_shape=jax.ShapeDtypeStruct((n, l, ATT_WIDTH), BF16),
        compiler_params=_params(("arbitrary", "arbitrary")),
        name="diff_attn_decode",
    )(page_table.reshape(-1), lam_vecs, q, k_new, v_new, bias, subln_g.reshape(1, ATT_DV),
      *([cache_k] * pps), *([cache_v] * pps))


_PEER_CAND = [(i, j) for i in range(PEER_NTOP) for j in range(PEER_NTOP) if (i + 1) * (j + 1) <= PEER_NTOP]


def _route_kernel(hn_ref, wq_ref, sk_ref, e2_o, ec_o, e1_o, q_scr, a_scr, b_scr):
    q_scr[...] = jnp.dot(hn_ref[...], wq_ref[...], preferred_element_type=F32)
    for h in range(PEER_HEADS):
        for m in range(2):
            c0 = (2 * h + m) * LANES
            qs = q_scr[:, c0:c0 + LANES].astype(BF16)
            s = lax.dot_general(sk_ref[m], qs, NT_DIMS, preferred_element_type=F32)
            (e1_o if m == 0 else e2_o)[h] = s
            top_scr = a_scr if m == 0 else b_scr
            x = s
            for r in range(PEER_NTOP):
                mx = jnp.max(x, axis=0, keepdims=True)
                top_scr[r, h:h + 1, :] = mx
                if r + 1 < PEER_NTOP:
                    x = jnp.where(x == mx, NEG_INF, x)
    a_top = [a_scr[r] for r in range(PEER_NTOP)]
    b_top = [b_scr[r] for r in range(PEER_NTOP)]
    cands = [a_top[i] + b_top[j] for (i, j) in _PEER_CAND]
    xs = list(cands)
    ranked = []
    for r in range(PEER_NTOP):
        mx = functools.reduce(jnp.maximum, xs)
        ranked.append(mx)
        if r + 1 < PEER_NTOP:
            xs = [jnp.where(x == mx, NEG_INF, x) for x in xs]
    thr = 0.5 * (ranked[PEER_TOPK - 1] + ranked[PEER_TOPK])
    top = a_top[0] + b_top[0]
    z = functools.reduce(lambda u, w: u + w, [jnp.where(c >= thr, jnp.exp(c - top), 0.0) for c in cands])
    scale = 0.5 / z
    for h in range(PEER_HEADS):
        s1 = e1_o[h]
        s2 = e2_o[h]
        a0 = a_top[0][h:h + 1]
        b0 = b_top[0][h:h + 1]
        e2_o[h] = jnp.exp(s2 - b0)
        ec_o[h] = jnp.exp((thr[h:h + 1] - b0) - s1)
        e1_o[h] = jnp.exp(s1 - a0) * scale[h:h + 1]


def peer_route(hn, wq, sk):
    t, d = hn.shape
    tm = min(256, t)
    spec = pl.BlockSpec((PEER_HEADS, PEER_NKEYS, tm), lambda i: (0, 0, i))
    shape = jax.ShapeDtypeStruct((PEER_HEADS, PEER_NKEYS, t), F32)
    return pl.pallas_call(
        _route_kernel,
        grid=(t // tm,),
        in_specs=[pl.BlockSpec((tm, d), lambda i: (i, 0)),
                  pl.BlockSpec(wq.shape, lambda i: (0, 0)),
                  pl.BlockSpec(sk.shape, lambda i: (0, 0, 0))],
        out_specs=[spec, spec, spec],
        out_shape=[shape, shape, shape],
        scratch_shapes=[pltpu.VMEM((tm, wq.shape[1]), F32),
                        pltpu.VMEM((PEER_NTOP, PEER_HEADS, tm), F32),
                        pltpu.VMEM((PEER_NTOP, PEER_HEADS, tm), F32)],
        compiler_params=_params(("arbitrary",)),
        name="peer_route",
    )(hn, wq, sk)


GELU_C0 = math.sqrt(2.0 / math.pi)
GELU_C1 = GELU_C0 * 0.044715


def _peer_kernel(hn_ref, u_ref, vt_ref, e2_ref, ec_ref, e1_ref, x1_ref, g2_ref, o_ref, acc_scr, act_scr, wa_scr):
    e = pl.program_id(2)

    @pl.when(e == 0)
    def _():
        acc_scr[...] = jnp.zeros(acc_scr.shape, F32)

    act_scr[...] = lax.dot_general(u_ref[...], hn_ref[...], NT_DIMS, preferred_element_type=F32)
    for il in range(PEER_IB):
        rows = slice(il * PEER_NKEYS, (il + 1) * PEER_NKEYS)
        w = None
        for h in range(PEER_HEADS):
            e2 = e2_ref[h]
            term = jnp.where(e2 >= ec_ref[h, il:il + 1, :], e2 * e1_ref[h, il:il + 1, :], 0.0)
            w = term if w is None else w + term
        x = act_scr[rows, :]
        inner = x * (GELU_C0 + GELU_C1 * (x * x))
        wa_scr[rows, :] = (w * x * (1.0 + jnp.tanh(inner))).astype(BF16)
    acc_scr[...] += jnp.dot(vt_ref[...], wa_scr[...], preferred_element_type=F32)

    @pl.when(e == pl.num_programs(2) - 1)
    def _():
        y = acc_scr[...].T
        o_ref[...] = x1_ref[...] + g2_ref[...] * y.reshape(x1_ref.shape)


def peer_mix(hn, u_b, vt_b, e2t, ect, e1t, x1, g2):
    n, l, d = x1.shape
    nb, lb = _row_blocks(n, l)
    tm = nb * lb
    lk = l // lb
    ne = PEER_EXPERTS // PEER_EB
    tok = lambda i, k, e: (i * lk + k, 0)
    return pl.pallas_call(
        _peer_kernel,
        grid=(n // nb, lk, ne),
        in_specs=[pl.BlockSpec((tm, d), tok),
                  pl.BlockSpec((PEER_EB, d), lambda i, k, e: (e, 0)),
                  pl.BlockSpec((d, PEER_EB), lambda i, k, e: (0, e)),
                  pl.BlockSpec((PEER_HEADS, PEER_NKEYS, tm), lambda i, k, e: (0, 0, i * lk + k)),
                  pl.BlockSpec((PEER_HEADS, PEER_IB, tm), lambda i, k, e: (0, e, i * lk + k)),
                  pl.BlockSpec((PEER_HEADS, PEER_IB, tm), lambda i, k, e: (0, e, i * lk + k)),
                  pl.BlockSpec((nb, lb, d), lambda i, k, e: (i, k, 0)),
                  pl.BlockSpec((nb, 1, d), lambda i, k, e: (i, 0, 0))],
        out_specs=pl.BlockSpec((nb, lb, d), lambda i, k, e: (i, k, 0)),
        out_shape=jax.ShapeDtypeStruct((n, l, d), F32),
        scratch_shapes=[pltpu.VMEM((d, tm), F32), pltpu.VMEM((PEER_EB, tm), F32),
                        pltpu.VMEM((PEER_EB, tm), BF16)],
        compiler_params=_params(("arbitrary", "arbitrary", "arbitrary")),
        name="peer_mix",
    )(hn, u_b, vt_b, e2t, ect, e1t, x1, g2)


def peer_ffn_residual(hn, x1, g2, wq_b, sk_b, u_b, vt_b):
    e2t, ect, e1t = peer_route(hn, wq_b, sk_b)
    return peer_mix(hn, u_b, vt_b, e2t, ect, e1t, x1, g2)


def _trunk(x, mods, pos0, conv_prev, ssm_prev, pool_prev, paged_kv, p, w):
    n, l, d = x.shape
    t = n * l
    sh1, sc1, g1, sh2, sc2, g2 = mods[0]
    proj = norm_inproj(x, p["norm_mix_g"][0], sc1, sh1, w["hyb_in"], tn=HYB_PROJ // 3)
    mixed, conv_new, ssm_new, pool_new = ssd_pool(proj, n, l, conv_prev, ssm_prev, pool_prev, pos0, w["ssm"])
    x1, hn = outproj_residual(mixed, w["hyb_out"], x, g1, p["norm_ffn_g"][0], sc2, sh2)
    x2 = peer_ffn_residual(hn, x1, g2, w["peer_wq"][0], w["peer_sk"][0], w["peer_u"][0], w["peer_vt"][0])

    sh1, sc1, g1, sh2, sc2, g2 = mods[1]
    qb, kf, kb, vf, vb = norm_qkv(x2, p["norm_mix_g"][1], sc1, sh1, w["att_in"], p["att_q_g"][0], p["att_k_g"][0])
    lam_init = 0.8 - 0.6 * math.exp(-0.3 * 1)
    lam_vecs = jnp.stack([p["att_lambda_q1"][0], p["att_lambda_k1"][0],
                          p["att_lambda_q2"][0], p["att_lambda_k2"][0]])
    if paged_kv is None:
        tq = min(256, l)
        bias = bias_tiles(p["rel_bias"], (0, tq), tq, tq)
        o = diff_attention_prompt(qb, kb, vb, n, l, bias, lam_vecs, p["att_subln_g"][0], lam_init)
    else:
        cache_k, cache_v, page_table = paged_kv
        past_len = page_table.shape[1] * PAGE
        bt = bias_tiles(p["rel_bias"], (pos0 - (past_len - PAGE), pos0 - past_len), l, PAGE)
        bias = jnp.broadcast_to(jnp.transpose(bt, (1, 0, 2, 3))[:, :, None], (2, ATT_HEADS, 2, l, PAGE))
        bias = bias.reshape(2, 2 * ATT_HEADS * l, PAGE)
        o = diff_attention_decode(qb.reshape(n, l, ATT_WIDTH), kf.reshape(n, l, ATT_WIDTH),
                                  vf.reshape(n, l, ATT_WIDTH), cache_k, cache_v, page_table,
                                  bias, lam_vecs, p["att_subln_g"][0], lam_init).reshape(t, ATT_WIDTH)
    x3, hn = outproj_residual(o, w["att_out"], x2, g1, p["norm_ffn_g"][1], sc2, sh2)
    x4 = peer_ffn_residual(hn, x3, g2, w["peer_wq"][1], w["peer_sk"][1], w["peer_u"][1], w["peer_vt"][1])
    k_new = kf.reshape(1, n, l, ATT_HEADS, 2 * ATT_DH)
    v_new = vf.reshape(1, n, l, ATT_HEADS, ATT_DV)
    return x4, conv_new[None], ssm_new[None], pool_new[None], k_new, v_new


def kernel(x_prompt, x_sample, state_conv, state_ssm, state_pool, cache_k, cache_v, page_table,
           c_prompt, c_sample, rel_bias, norm_mix_g, norm_ffn_g, ada_w, ada_b,
           hyb_w_in, ssm_conv_w, ssm_conv_b, ssm_dt_bias, ssm_a_log, ssm_d, ssm_norm_g,
           pool_w, pool_scale, hyb_w_out, att_w_in, att_q_g, att_k_g,
           att_lambda_q1, att_lambda_k1, att_lambda_q2, att_lambda_k2, att_subln_g, att_w_out,
           peer_wq, peer_subkeys, peer_u, peer_v):
    p = {
        "rel_bias": rel_bias, "norm_mix_g": norm_mix_g, "norm_ffn_g": norm_ffn_g,
        "att_q_g": att_q_g, "att_k_g": att_k_g,
        "att_lambda_q1": att_lambda_q1, "att_lambda_k1": att_lambda_k1,
        "att_lambda_q2": att_lambda_q2, "att_lambda_k2": att_lambda_k2, "att_subln_g": att_subln_g,
    }
    w_in = hyb_w_in[0]
    z_w = w_in[:, :SSM_D_INNER]
    xbc_w = w_in[:, SSM_D_INNER:SSM_D_INNER + SSM_CONV_DIM]
    dt_w = w_in[:, SSM_D_INNER + SSM_CONV_DIM:SSM_D_INNER + SSM_CONV_DIM + SSM_HEADS]
    u_w = w_in[:, SSM_D_INNER + SSM_CONV_DIM + SSM_HEADS:]
    hyb_in = jnp.concatenate([z_w, xbc_w, u_w, jnp.pad(dt_w, ((0, 0), (0, LANES - SSM_HEADS)))], axis=1)
    w = {
        "hyb_in": hyb_in.astype(BF16),
        "hyb_out": hyb_w_out[0].astype(BF16),
        "att_in": att_w_in[0].astype(BF16),
        "att_out": att_w_out[0].astype(BF16),
        "peer_wq": peer_wq.astype(BF16),
        "peer_sk": peer_subkeys.astype(BF16),
        "peer_u": peer_u.astype(BF16),
        "peer_vt": jnp.swapaxes(peer_v, 1, 2).astype(BF16),
        "ssm": {"ssm_conv_w": ssm_conv_w[0], "ssm_conv_b": ssm_conv_b[0], "ssm_dt_bias": ssm_dt_bias[0],
                "ssm_a_log": ssm_a_log[0], "ssm_d": ssm_d[0], "ssm_norm_g": ssm_norm_g[0],
                "pool_w": pool_w[0], "pool_scale": pool_scale[0]},
    }
    nb, ns = x_prompt.shape[0], x_sample.shape[0]
    d = x_prompt.shape[2]
    rows = nb + ns
    rpad = -rows % SUBLANES
    c_all = jnp.concatenate([c_prompt, c_sample, jnp.zeros((rpad, d), F32)], axis=0)
    mod = ada_mod(c_all, ada_w, ada_b)

    def mods_for(lo, cnt):
        return [tuple(mod[i, lo:lo + cnt, k * d:(k + 1) * d].reshape(cnt, 1, d) for k in range(6))
                for i in range(mod.shape[0])]

    conv0 = jnp.zeros((nb, SSM_CONV - 1, SSM_CONV_DIM), F32)
    ssm0 = jnp.zeros((nb, SSM_HEADS, SSM_HEAD_DIM, SSM_D_STATE), F32)
    pool0 = jnp.zeros((nb, POOL_HIST, POOL_DIM), F32)
    yp, conv_p, ssm_p, pool_p, k_p, v_p = _trunk(x_prompt, mods_for(0, nb), 0, conv0, ssm0, pool0, None, p, w)
    past_len = page_table.shape[1] * cache_k.shape[2]
    n_phys = cache_k.shape[1]
    ys, conv_s, ssm_s, pool_s, k_s, v_s = _trunk(
        x_sample, mods_for(nb, ns), past_len, state_conv[0], state_ssm[0], state_pool[0],
        (cache_k[0].reshape(n_phys, PAGE, ATT_WIDTH), cache_v[0].reshape(n_phys, PAGE, ATT_WIDTH), page_table),
        p, w)
    return (yp, ys, conv_p, ssm_p, pool_p, k_p, v_p, conv_s, ssm_s, pool_s, k_s, v_s)
```

```python
import functools
import math

import numpy as np
import jax
import jax.numpy as jnp
from jax import lax
from jax.experimental import pallas as pl
from jax.experimental.pallas import tpu as pltpu

F32 = jnp.float32
BF16 = jnp.bfloat16

D_MODEL = 1024
EPS = 1e-6
LANES = 128
SUBLANES = 8

SSM_D_INNER = 2048
SSM_HEAD_DIM = 64
SSM_HEADS = 32
SSM_GROUPS = 2
SSM_D_STATE = 128
SSM_CONV = 4
SSM_CHUNK = 128
SSM_CONV_DIM = 2560
GROUP_W = SSM_D_INNER // SSM_GROUPS

POOL_WINDOWS = (2, 4, 8, 16)
POOL_DIM = 1024
POOL_GROUP = 256
POOL_HIST = 15
POOL_HALO = 16
CONV_HALO = 8

HYB_PROJ = SSM_D_INNER + SSM_CONV_DIM + POOL_DIM + LANES
HYB_MIX = SSM_D_INNER + POOL_DIM

ATT_HEADS = 8
ATT_DH = 64
ATT_DV = 128
ATT_WIDTH = 1024
REL_BUCKETS = 32
REL_MAX_DIST = 128
PAGE = 128
ATT_TQ = 512
DECODE_PAGES_PER_STEP = 8

PEER_HEADS = 8
PEER_NKEYS = 128
PEER_EXPERTS = PEER_NKEYS * PEER_NKEYS
PEER_TOPK = 16
PEER_NTOP = PEER_TOPK + 1
PEER_EB = 1024
PEER_IB = PEER_EB // PEER_NKEYS
PEER_SCORE_CHUNK = 2

NEG_INF = float("-inf")
NT_DIMS = (((1,), (1,)), ((), ()))

VMEM_LIMIT = 56 * 1024 * 1024


def _params(sem):
    return pltpu.CompilerParams(dimension_semantics=sem, vmem_limit_bytes=VMEM_LIMIT)


def _sigmoid(x):
    return 1.0 / (1.0 + jnp.exp(-x))


def _silu(x):
    return x * _sigmoid(x)


def _split3(x):
    hi = x.astype(BF16)
    r1 = x - hi.astype(F32)
    mid = r1.astype(BF16)
    lo = (r1 - mid.astype(F32)).astype(BF16)
    return hi, mid, lo


def _norm_mod(x3, g_ref, sc_ref, sh_ref):
    ms = jnp.mean(x3 * x3, axis=-1, keepdims=True)
    y = x3 * lax.rsqrt(ms + EPS) * g_ref[...]
    return y * (1.0 + sc_ref[...]) + sh_ref[...]


def _row_blocks(n, l):
    if l >= 512:
        return 1, 512
    if l >= SUBLANES and l % SUBLANES == 0 and l < 512:
        if l == SUBLANES:
            nb = min(n, 64)
            while n % nb:
                nb //= 2
            return nb, l
        return 1, l
    raise ValueError(f"unsupported sequence length {l}")


def _ada_kernel(c_ref, w_ref, b_ref, o_ref):
    ca = _silu(c_ref[...]).astype(BF16)
    o_ref[0] = jnp.dot(ca, w_ref[0].astype(BF16), preferred_element_type=F32) + b_ref[0]


def ada_mod(c_all, ada_w, ada_b):
    r = c_all.shape[0]
    depth, d, n6 = ada_w.shape
    tn = 512
    return pl.pallas_call(
        _ada_kernel,
        grid=(depth, n6 // tn),
        in_specs=[pl.BlockSpec((r, d), lambda i, j: (0, 0)),
                  pl.BlockSpec((1, d, tn), lambda i, j: (i, 0, j)),
                  pl.BlockSpec((1, 1, tn), lambda i, j: (i, 0, j))],
        out_specs=pl.BlockSpec((1, r, tn), lambda i, j: (i, 0, j)),
        out_shape=jax.ShapeDtypeStruct((depth, r, n6), F32),
        compiler_params=_params(("arbitrary", "arbitrary")),
        name="ada_mod",
    )(c_all, ada_w, ada_b.reshape(depth, 1, n6))


def _inproj_kernel(x_ref, g_ref, sc_ref, sh_ref, w_ref, o_ref, h_scr):
    @pl.when(pl.program_id(2) == 0)
    def _():
        h = _norm_mod(x_ref[...], g_ref, sc_ref, sh_ref)
        h_scr[...] = h.reshape(h_scr.shape).astype(BF16)

    o_ref[...] = jnp.dot(h_scr[...], w_ref[...], preferred_element_type=F32)


def norm_inproj(x, g, sc, sh, w, tn):
    n, l, d = x.shape
    nb, lb = _row_blocks(n, l)
    m = nb * lb
    lk = l // lb
    nout = w.shape[1]
    return pl.pallas_call(
        _inproj_kernel,
        grid=(n // nb, lk, nout // tn),
        in_specs=[pl.BlockSpec((nb, lb, d), lambda i, k, j: (i, k, 0)),
                  pl.BlockSpec((1, d), lambda i, k, j: (0, 0)),
                  pl.BlockSpec((nb, 1, d), lambda i, k, j: (i, 0, 0)),
                  pl.BlockSpec((nb, 1, d), lambda i, k, j: (i, 0, 0)),
                  pl.BlockSpec((d, tn), lambda i, k, j: (0, j))],
        out_specs=pl.BlockSpec((m, tn), lambda i, k, j: (i * lk + k, j)),
        out_shape=jax.ShapeDtypeStruct((n * l, nout), F32),
        scratch_shapes=[pltpu.VMEM((m, d), BF16)],
        compiler_params=_params(("arbitrary", "arbitrary", "arbitrary")),
        name="norm_inproj",
    )(x, g.reshape(1, d), sc, sh, w)


def _seg_rms(y, g128, bd):
    sq = y * y
    hi = sq.astype(BF16)
    lo = (sq - hi.astype(F32)).astype(BF16)
    outs = []
    for cb in range(y.shape[1] // LANES):
        sl = slice(cb * LANES, (cb + 1) * LANES)
        ms = (jnp.dot(hi[:, sl], bd, preferred_element_type=F32)
              + jnp.dot(lo[:, sl], bd, preferred_element_type=F32))
        outs.append(y[:, sl] * lax.rsqrt(ms + EPS) * g128)
    return jnp.concatenate(outs, axis=1)


def _qkv_kernel(x_ref, g_ref, sc_ref, sh_ref, w_ref, qg_ref, kg_ref, bd_ref,
                q_o, kf_o, kb_o, vf_o, vt_o, h_scr):
    j = pl.program_id(2)

    @pl.when(j == 0)
    def _():
        h = _norm_mod(x_ref[...], g_ref, sc_ref, sh_ref)
        h_scr[...] = h.reshape(h_scr.shape).astype(BF16)

    y = jnp.dot(h_scr[...], w_ref[...], preferred_element_type=F32)

    @pl.when(j == 0)
    def _():
        q_o[...] = (_seg_rms(y, qg_ref[...], bd_ref[...]) * (ATT_DH ** -0.5)).astype(BF16)

    @pl.when(j == 1)
    def _():
        kn = _seg_rms(y, kg_ref[...], bd_ref[...])
        kf_o[...] = kn
        kb_o[...] = kn.astype(BF16)

    @pl.when(j == 2)
    def _():
        vf_o[...] = y
        vt_o[...] = y.T.astype(BF16)


def norm_qkv(x, g, sc, sh, w, q_g, k_g):
    n, l, d = x.shape
    nb, lb = _row_blocks(n, l)
    m = nb * lb
    lk = l // lb
    t = n * l
    seg = np.arange(LANES) // ATT_DH
    bd = jnp.asarray((seg[:, None] == seg[None, :]).astype(np.float32) / ATT_DH, BF16)
    row = lambda i, k, j: (i * lk + k, 0)
    const2 = lambda i, k, j: (0, 0)
    outs = pl.pallas_call(
        _qkv_kernel,
        grid=(n // nb, lk, 3),
        in_specs=[pl.BlockSpec((nb, lb, d), lambda i, k, j: (i, k, 0)),
                  pl.BlockSpec((1, d), const2),
                  pl.BlockSpec((nb, 1, d), lambda i, k, j: (i, 0, 0)),
                  pl.BlockSpec((nb, 1, d), lambda i, k, j: (i, 0, 0)),
                  pl.BlockSpec((d, ATT_WIDTH), lambda i, k, j: (0, j)),
                  pl.BlockSpec((1, LANES), const2),
                  pl.BlockSpec((1, LANES), const2),
                  pl.BlockSpec((LANES, LANES), const2)],
        out_specs=[pl.BlockSpec((m, ATT_WIDTH), row)] * 4
                  + [pl.BlockSpec((ATT_WIDTH, m), lambda i, k, j: (0, i * lk + k))],
        out_shape=[jax.ShapeDtypeStruct((t, ATT_WIDTH), BF16),
                   jax.ShapeDtypeStruct((t, ATT_WIDTH), F32),
                   jax.ShapeDtypeStruct((t, ATT_WIDTH), BF16),
                   jax.ShapeDtypeStruct((t, ATT_WIDTH), F32),
                   jax.ShapeDtypeStruct((ATT_WIDTH, t), BF16)],
        scratch_shapes=[pltpu.VMEM((m, d), BF16)],
        compiler_params=_params(("arbitrary", "arbitrary", "arbitrary")),
        name="norm_qkv",
    )(x, g.reshape(1, d), sc, sh, w,
      jnp.tile(q_g, 2).reshape(1, LANES), jnp.tile(k_g, 2).reshape(1, LANES), bd)
    return outs


def _outproj_kernel(a_ref, w_ref, x_ref, g1_ref, gn_ref, sc_ref, sh_ref, x1_o, hn_o):
    y = jnp.dot(a_ref[...], w_ref[...], preferred_element_type=F32)
    x1 = x_ref[...] + g1_ref[...] * y.reshape(x_ref.shape)
    x1_o[...] = x1
    hn_o[...] = _norm_mod(x1, gn_ref, sc_ref, sh_ref).reshape(hn_o.shape).astype(BF16)


def outproj_residual(a, w, x, g1, gn, sc, sh):
    n, l, d = x.shape
    nb, lb = _row_blocks(n, l)
    m = nb * lb
    lk = l // lb
    kdim = a.shape[1]
    seq = lambda i, k: (i, 0, 0)
    return pl.pallas_call(
        _outproj_kernel,
        grid=(n // nb, lk),
        in_specs=[pl.BlockSpec((m, kdim), lambda i, k: (i * lk + k, 0)),
                  pl.BlockSpec((kdim, d), lambda i, k: (0, 0)),
                  pl.BlockSpec((nb, lb, d), lambda i, k: (i, k, 0)),
                  pl.BlockSpec((nb, 1, d), seq),
                  pl.BlockSpec((1, d), lambda i, k: (0, 0)),
                  pl.BlockSpec((nb, 1, d), seq),
                  pl.BlockSpec((nb, 1, d), seq)],
        out_specs=[pl.BlockSpec((nb, lb, d), lambda i, k: (i, k, 0)),
                   pl.BlockSpec((m, d), lambda i, k: (i * lk + k, 0))],
        out_shape=[jax.ShapeDtypeStruct((n, l, d), F32),
                   jax.ShapeDtypeStruct((n * l, d), BF16)],
        compiler_params=_params(("arbitrary", "arbitrary")),
        name="outproj_residual",
    )(a, w, x, g1, gn.reshape(1, d), sc, sh)


def _ssd_pool_kernel(proj_ref, convp_ref, ssmp_ref, poolp_ref, convw_ref, convb_ref, dtb_ref, alog_ref,
                     dexp_ref, normg_ref, poolw_ref, pools_ref, tril_ref, rexp_ref,
                     mixed_o, convn_o, ssmn_o, pooln_o,
                     ext_c, ext_p, ht_scr, *, lr, pos0):
    q = SSM_CHUNK
    c = pl.program_id(1)
    nc = pl.num_programs(1)

    @pl.when(c == 0)
    def _():
        ext_c[0:CONV_HALO, :] = convp_ref[0]
        ext_p[0:POOL_HALO, :] = poolp_ref[0]
        for g in range(SSM_GROUPS):
            ht_scr[g] = ssmp_ref[0, g * GROUP_W:(g + 1) * GROUP_W, :].T

    xbc_cols = slice(SSM_D_INNER, SSM_D_INNER + SSM_CONV_DIM)
    u_cols = slice(SSM_D_INNER + SSM_CONV_DIM, SSM_D_INNER + SSM_CONV_DIM + POOL_DIM)
    dt_cols = slice(SSM_D_INNER + SSM_CONV_DIM + POOL_DIM, HYB_PROJ)

    ext_c[CONV_HALO:CONV_HALO + lr, :] = proj_ref[:, xbc_cols]
    ext_p[POOL_HALO:POOL_HALO + lr, :] = proj_ref[:, u_cols]
    if lr < q:
        ext_c[CONV_HALO + lr:CONV_HALO + q, :] = jnp.zeros((q - lr, SSM_CONV_DIM), F32)
        ext_p[POOL_HALO + lr:POOL_HALO + q, :] = jnp.zeros((q - lr, POOL_DIM), F32)

    acc = convb_ref[...] + ext_c[pl.ds(CONV_HALO - 3, q), :] * convw_ref[0:1, :]
    for tap in range(1, SSM_CONV):
        acc = acc + ext_c[pl.ds(CONV_HALO - 3 + tap, q), :] * convw_ref[tap:tap + 1, :]
    xbc = _silu(acc)
    xs = xbc[:, :SSM_D_INNER]

    row = lax.broadcasted_iota(jnp.int32, (q, LANES), 0)
    dt_raw = proj_ref[:, dt_cols] + dtb_ref[...]
    if lr < q:
        dt_raw = jnp.concatenate([dt_raw, jnp.zeros((q - lr, LANES), F32)], axis=0)
    dt = jnp.maximum(dt_raw, 0.0) + jnp.log(1.0 + jnp.exp(-jnp.abs(dt_raw)))
    dt = jnp.where(row < lr, dt, 0.0)
    a_neg = -jnp.exp(alog_ref[...])
    d_a = dt * a_neg

    tril = tril_ref[...]
    rexp = rexp_ref[...]
    a_cs = sum(jnp.dot(tril, p, preferred_element_type=F32) for p in _split3(d_a))
    acs_e = sum(jnp.dot(p, rexp, preferred_element_type=F32) for p in _split3(a_cs))
    dt_e = sum(jnp.dot(p, rexp, preferred_element_type=F32) for p in _split3(dt))
    a_cs_t = a_cs.T

    xdt = xs * dt_e
    alast_e = acs_e[lr - 1:lr, :]
    xdtw = (xdt * jnp.exp(alast_e - acs_e)).astype(BF16)
    ea_e = jnp.exp(acs_e)
    xdt_b = xdt.astype(BF16)

    ii = lax.broadcasted_iota(jnp.int32, (q, q), 0)
    jj = lax.broadcasted_iota(jnp.int32, (q, q), 1)
    causal = ii >= jj
    lane = lax.broadcasted_iota(jnp.int32, (q, LANES), 1)
    even_half = lane < SSM_HEAD_DIM
    zero_b = jnp.zeros((q, LANES), BF16)

    ys = []
    for g in range(SSM_GROUPS):
        bg = xbc[:, SSM_D_INNER + g * SSM_D_STATE:SSM_D_INNER + (g + 1) * SSM_D_STATE]
        cg = xbc[:, SSM_D_INNER + SSM_GROUPS * SSM_D_STATE + g * SSM_D_STATE:
                 SSM_D_INNER + SSM_GROUPS * SSM_D_STATE + (g + 1) * SSM_D_STATE].astype(BF16)
        bg_t = bg.T.astype(BF16)
        gcols = slice(g * GROUP_W, (g + 1) * GROUP_W)
        cb = jnp.dot(cg, bg_t, preferred_element_type=F32)
        h_in = ht_scr[g]
        y_off = jnp.dot(cg, h_in.astype(BF16), preferred_element_type=F32) * ea_e[:, gcols]
        st = jnp.dot(bg_t, xdtw[:, gcols], preferred_element_type=F32)
        ht_scr[g] = h_in * jnp.exp(alast_e[:, gcols]) + st
        pieces = []
        for pr in range(GROUP_W // LANES):
            ms = []
            for hh in range(2):
                h = g * (SSM_HEADS // SSM_GROUPS) + 2 * pr + hh
                seg = a_cs[:, h:h + 1] - a_cs_t[h:h + 1, :]
                ms.append((cb * jnp.exp(jnp.where(causal, seg, NEG_INF))).astype(BF16))
            xblk = xdt_b[:, g * GROUP_W + pr * LANES:g * GROUP_W + (pr + 1) * LANES]
            rhs = jnp.concatenate([jnp.where(even_half, xblk, zero_b),
                                   jnp.where(even_half, zero_b, xblk)], axis=0)
            pieces.append(jnp.dot(jnp.concatenate(ms, axis=1), rhs, preferred_element_type=F32))
        ys.append(jnp.concatenate(pieces, axis=1) + y_off)
    y = jnp.concatenate(ys, axis=1) + dexp_ref[...] * xs
    y = y * _silu(proj_ref[:, :SSM_D_INNER]) if lr == q else (
        y * _silu(jnp.concatenate([proj_ref[:, :SSM_D_INNER], jnp.zeros((q - lr, SSM_D_INNER), F32)], axis=0)))
    for g in range(SSM_GROUPS):
        gcols = slice(g * GROUP_W, (g + 1) * GROUP_W)
        yg = y[:, gcols]
        ms = jnp.mean(yg * yg, axis=-1, keepdims=True)
        yn = yg * lax.rsqrt(ms + EPS) * normg_ref[:, gcols]
        mixed_o[:, gcols] = yn[:lr].astype(BF16)

    pos = pos0 + c * q + lax.broadcasted_iota(jnp.int32, (q, POOL_GROUP), 0)
    for gi, w in enumerate(POOL_WINDOWS):
        cols = slice(gi * POOL_GROUP, (gi + 1) * POOL_GROUP)
        cur = ext_p[pl.ds(POOL_HALO, q), cols]
        wsum = cur
        for k in range(1, w):
            wsum = wsum + ext_p[pl.ds(POOL_HALO - k, q), cols]
        cnt = jnp.minimum(w, pos + 1).astype(F32)
        pooled = wsum / cnt - cur
        yp = jnp.dot(pooled.astype(BF16), poolw_ref[gi], preferred_element_type=F32) * pools_ref[:, cols]
        mixed_o[:, SSM_D_INNER + gi * POOL_GROUP:SSM_D_INNER + (gi + 1) * POOL_GROUP] = yp[:lr].astype(BF16)

    new_c = ext_c[lr:lr + CONV_HALO, :]
    new_p = ext_p[lr:lr + POOL_HALO, :]
    ext_c[0:CONV_HALO, :] = new_c
    ext_p[0:POOL_HALO, :] = new_p

    @pl.when(c == nc - 1)
    def _():
        convn_o[0] = new_c
        pooln_o[0] = new_p
        for g in range(SSM_GROUPS):
            ssmn_o[0, g * GROUP_W:(g + 1) * GROUP_W, :] = ht_scr[g].T


def ssd_pool(proj, n, l, conv_prev, ssm_prev, pool_prev, pos0, p):
    q = SSM_CHUNK
    if l % q == 0:
        lr, nc = q, l // q
    elif l < q and l % SUBLANES == 0:
        lr, nc = l, 1
    else:
        raise ValueError(f"unsupported sequence length {l}")
    convp = jnp.pad(conv_prev, ((0, 0), (CONV_HALO - (SSM_CONV - 1), 0), (0, 0)))
    poolp = jnp.pad(pool_prev, ((0, 0), (POOL_HALO - POOL_HIST, 0), (0, 0)))
    ssmp = ssm_prev.reshape(n, SSM_D_INNER, SSM_D_STATE)
    pad_h = LANES - SSM_HEADS
    tril = jnp.asarray(np.tril(np.ones((q, q), np.float32)), BF16)
    rexp_np = np.zeros((LANES, SSM_D_INNER), np.float32)
    rexp_np[np.arange(SSM_D_INNER) // SSM_HEAD_DIM, np.arange(SSM_D_INNER)] = 1.0
    rexp = jnp.asarray(rexp_np, BF16)
    c2 = lambda b, c: (0, 0)
    seq3 = lambda b, c: (b, 0, 0)
    kern = functools.partial(_ssd_pool_kernel, lr=lr, pos0=pos0)
    mixed, convn, ssmn, pooln = pl.pallas_call(
        kern,
        grid=(n, nc),
        in_specs=[pl.BlockSpec((lr, HYB_PROJ), lambda b, c: (b * nc + c, 0)),
                  pl.BlockSpec((1, CONV_HALO, SSM_CONV_DIM), seq3),
                  pl.BlockSpec((1, SSM_D_INNER, SSM_D_STATE), seq3),
                  pl.BlockSpec((1, POOL_HALO, POOL_DIM), seq3),
                  pl.BlockSpec((SSM_CONV, SSM_CONV_DIM), c2),
                  pl.BlockSpec((1, SSM_CONV_DIM), c2),
                  pl.BlockSpec((1, LANES), c2),
                  pl.BlockSpec((1, LANES), c2),
                  pl.BlockSpec((1, SSM_D_INNER), c2),
                  pl.BlockSpec((1, SSM_D_INNER), c2),
                  pl.BlockSpec((len(POOL_WINDOWS), POOL_GROUP, POOL_GROUP), lambda b, c: (0, 0, 0)),
                  pl.BlockSpec((1, POOL_DIM), c2),
                  pl.BlockSpec((q, q), c2),
                  pl.BlockSpec((LANES, SSM_D_INNER), c2)],
        out_specs=[pl.BlockSpec((lr, HYB_MIX), lambda b, c: (b * nc + c, 0)),
                   pl.BlockSpec((1, CONV_HALO, SSM_CONV_DIM), seq3),
                   pl.BlockSpec((1, SSM_D_INNER, SSM_D_STATE), seq3),
                   pl.BlockSpec((1, POOL_HALO, POOL_DIM), seq3)],
        out_shape=[jax.ShapeDtypeStruct((n * l, HYB_MIX), BF16),
                   jax.ShapeDtypeStruct((n, CONV_HALO, SSM_CONV_DIM), F32),
                   jax.ShapeDtypeStruct((n, SSM_D_INNER, SSM_D_STATE), F32),
                   jax.ShapeDtypeStruct((n, POOL_HALO, POOL_DIM), F32)],
        scratch_shapes=[pltpu.VMEM((CONV_HALO + q, SSM_CONV_DIM), F32),
                        pltpu.VMEM((POOL_HALO + q, POOL_DIM), F32),
                        pltpu.VMEM((SSM_GROUPS, SSM_D_STATE, GROUP_W), F32)],
        compiler_params=_params(("arbitrary", "arbitrary")),
        name="ssd_pool",
    )(proj, convp, ssmp, poolp,
      p["ssm_conv_w"], p["ssm_conv_b"].reshape(1, -1),
      jnp.pad(p["ssm_dt_bias"], (0, pad_h)).reshape(1, LANES),
      jnp.pad(p["ssm_a_log"], (0, pad_h)).reshape(1, LANES),
      jnp.repeat(p["ssm_d"], SSM_HEAD_DIM).reshape(1, SSM_D_INNER),
      p["ssm_norm_g"].reshape(1, SSM_D_INNER),
      p["pool_w"].astype(BF16), p["pool_scale"].reshape(1, POOL_DIM), tril, rexp)
    conv_new = convn[:, CONV_HALO - (SSM_CONV - 1):, :]
    ssm_new = ssmn.reshape(n, SSM_HEADS, SSM_HEAD_DIM, SSM_D_STATE)
    pool_new = pooln[:, POOL_HALO - POOL_HIST:, :]
    return mixed, conv_new, ssm_new, pool_new


def _bucket_bounds():
    max_exact = REL_BUCKETS // 2
    bounds = []
    for b in range(max_exact + 1, REL_BUCKETS):
        d = max_exact
        while True:
            val = math.log(d / max_exact) / math.log(REL_MAX_DIST / max_exact) * (REL_BUCKETS - max_exact)
            if max_exact + int(val) >= b:
                break
            d += 1
        bounds.append(d)
    return bounds


def _bias_kernel(rb_ref, o_ref, *, tiles, rows, cols, sign):
    h = pl.program_id(0)
    max_exact = REL_BUCKETS // 2
    bounds = _bucket_bounds()
    far = rb_ref[REL_BUCKETS - 1, h]
    ii = lax.broadcasted_iota(jnp.int32, (rows, cols), 0)
    jj = lax.broadcasted_iota(jnp.int32, (rows, cols), 1)
    for oi, (kind, off) in enumerate(tiles):
        if kind == "zero":
            o_ref[0, oi] = jnp.zeros((rows, cols), F32)
            continue
        if kind == "neg":
            o_ref[0, oi] = jnp.full((rows, cols), NEG_INF, F32)
            continue
        rel = off + sign * (ii - jj)
        dist = jnp.maximum(rel, 0)
        large = jnp.full((rows, cols), max_exact, jnp.int32)
        for bnd in bounds:
            large = large + (dist >= bnd).astype(jnp.int32)
        bucket = jnp.where(dist < max_exact, dist, large)
        bias = jnp.zeros((rows, cols), F32)
        for b in range(REL_BUCKETS):
            bias = bias + jnp.where(bucket == b, rb_ref[b, h] - far, 0.0)
        if kind == "bias_causal":
            bias = jnp.where(rel >= 0, bias, NEG_INF)
        o_ref[0, oi] = bias


def bias_tiles(rel_bias, tiles, rows, cols, sign=1):
    kern = functools.partial(_bias_kernel, tiles=tuple(tiles), rows=rows, cols=cols, sign=sign)
    return pl.pallas_call(
        kern,
        grid=(ATT_HEADS,),
        in_specs=[pl.BlockSpec(memory_space=pltpu.SMEM)],
        out_specs=pl.BlockSpec((1, len(tiles), rows, cols), lambda h: (h, 0, 0, 0)),
        out_shape=jax.ShapeDtypeStruct((ATT_HEADS, len(tiles), rows, cols), F32),
        compiler_params=_params(("arbitrary",)),
        name="bias_tiles",
    )(rel_bias)


def _lambda(lam_ref, lam_init):
    l1 = jnp.sum(lam_ref[0:1, :] * lam_ref[1:2, :], axis=-1, keepdims=True)
    l2 = jnp.sum(lam_ref[2:3, :] * lam_ref[3:4, :], axis=-1, keepdims=True)
    return jnp.exp(l1) - jnp.exp(l2) + lam_init


def _subln(o, sg_ref, lam_init):
    ms = jnp.mean(o * o, axis=-1, keepdims=True)
    return o * lax.rsqrt(ms + EPS) * sg_ref[...] * (1.0 - lam_init)


def _attn_kernel(lam_ref, q_ref, k_ref, vt_ref, bias_ref, sgt_ref, o_ref,
                 m_scr, l_scr, acc_scr, s_a, s_b, p_a, p_b, al_a, al_b, *, tq, lam_init):
    qi = pl.program_id(2)
    q = q_ref[...]
    lane = lax.broadcasted_iota(jnp.int32, (tq, LANES), 1)
    zero = jnp.zeros_like(q)
    q_halves = (jnp.where(lane < ATT_DH, q, zero), jnp.where(lane < ATT_DH, zero, q))
    m_scr[...] = jnp.full(m_scr.shape, NEG_INF, F32)
    l_scr[...] = jnp.zeros(l_scr.shape, F32)
    acc_scr[...] = jnp.zeros(acc_scr.shape, F32)
    p_b[...] = jnp.zeros(p_b.shape, BF16)
    al_b[...] = jnp.ones(al_b.shape, F32)

    def block_offset(kb):
        return pl.multiple_of(jnp.clip(kb, 0, qi) * tq, tq)

    def scores(kb, s_ref):
        k = k_ref[pl.ds(block_offset(kb), tq), :]
        for mi in range(2):
            s_ref[mi] = lax.dot_general(k, q_halves[mi], NT_DIMS, preferred_element_type=F32)

    def softmax(kb, s_ref, p_ref, al_ref):
        bias_t = bias_ref[0, jnp.clip(kb - qi + 2, 0, 3)]
        for mi in range(2):
            s = s_ref[mi] + bias_t
            m_old = m_scr[mi]
            m_new = jnp.maximum(m_old, jnp.max(s, axis=0, keepdims=True))
            alpha = jnp.exp(m_old - m_new)
            pr = jnp.exp(s - m_new)
            l_scr[mi] = alpha * l_scr[mi] + jnp.sum(pr, axis=0, keepdims=True)
            m_scr[mi] = m_new
            p_ref[mi] = pr.astype(BF16)
            al_ref[mi] = alpha

    def values(kb, p_ref, al_ref):
        vt = vt_ref[:, pl.ds(block_offset(kb), tq)]
        for mi in range(2):
            acc_scr[mi] = al_ref[mi] * acc_scr[mi] + jnp.dot(vt, p_ref[mi], preferred_element_type=F32)

    def pair(j, carry):
        a = 2 * j
        scores(a + 1, s_b)
        softmax(a, s_a, p_a, al_a)
        values(a - 1, p_b, al_b)
        scores(a + 2, s_a)
        softmax(a + 1, s_b, p_b, al_b)
        values(a, p_a, al_a)
        return carry

    n_pairs = (qi + 2) // 2
    scores(0, s_a)
    lax.fori_loop(0, n_pairs, pair, 0)
    values(2 * n_pairs - 1, p_b, al_b)
    lam = _lambda(lam_ref, lam_init)
    o_t = acc_scr[0] / l_scr[0] - lam * (acc_scr[1] / l_scr[1])
    ms = jnp.mean(o_t * o_t, axis=0, keepdims=True)
    o_t = o_t * lax.rsqrt(ms + EPS) * sgt_ref[...] * (1.0 - lam_init)
    o_ref[...] = o_t.T.astype(BF16)


def diff_attention_prompt(q, k, vt, n, l, bias_t, lam_vecs, subln_g, lam_init, tq):
    nq = l // tq
    kern = functools.partial(_attn_kernel, tq=tq, lam_init=lam_init)
    sg_t = jnp.broadcast_to(subln_g.reshape(ATT_DV, 1), (ATT_DV, tq))
    return pl.pallas_call(
        kern,
        grid=(n, ATT_HEADS, nq),
        in_specs=[pl.BlockSpec((4, ATT_DH), lambda b, h, i: (0, 0)),
                  pl.BlockSpec((tq, LANES), lambda b, h, i: (b * nq + i, h)),
                  pl.BlockSpec((l, LANES), lambda b, h, i: (b, h)),
                  pl.BlockSpec((LANES, l), lambda b, h, i: (h, b)),
                  pl.BlockSpec((1, 4, tq, tq), lambda b, h, i: (h, 0, 0, 0)),
                  pl.BlockSpec((ATT_DV, tq), lambda b, h, i: (0, 0))],
        out_specs=pl.BlockSpec((tq, LANES), lambda b, h, i: (b * nq + i, h)),
        out_shape=jax.ShapeDtypeStruct((n * l, ATT_WIDTH), BF16),
        scratch_shapes=[pltpu.VMEM((2, 1, tq), F32), pltpu.VMEM((2, 1, tq), F32),
                        pltpu.VMEM((2, ATT_DV, tq), F32),
                        pltpu.VMEM((2, tq, tq), F32), pltpu.VMEM((2, tq, tq), F32),
                        pltpu.VMEM((2, tq, tq), BF16), pltpu.VMEM((2, tq, tq), BF16),
                        pltpu.VMEM((2, 1, tq), F32), pltpu.VMEM((2, 1, tq), F32)],
        compiler_params=_params(("arbitrary", "arbitrary", "arbitrary")),
        name="diff_attn_prompt",
    )(lam_vecs, q, k, vt, bias_t, sg_t)


def _attn_decode_kernel(pt_ref, lam_ref, q_ref, kn_ref, vn_ref, pmask_ref, nmask_ref, sg_ref, *rest,
                        pps, l, lam_init):
    k_refs = rest[:pps]
    v_refs = rest[pps:2 * pps]
    o_ref = rest[2 * pps]
    q2_scr, m_scr, l_scr, acc_scr = rest[2 * pps + 1:]
    s_idx = pl.program_id(1)
    ns = pl.num_programs(1)
    last = s_idx == ns - 1

    @pl.when(s_idx == 0)
    def _():
        qf = q_ref[0].astype(F32)
        lane = lax.broadcasted_iota(jnp.int32, (l, LANES), 1)
        pieces = []
        for h in range(ATT_HEADS):
            qh = qf[:, h * LANES:(h + 1) * LANES]
            pieces += [jnp.where(lane < ATT_DH, qh, 0.0), jnp.where(lane < ATT_DH, 0.0, qh)]
        q2_scr[...] = jnp.concatenate(pieces, axis=0).astype(BF16)
        m_scr[...] = jnp.full(m_scr.shape, NEG_INF, F32)
        l_scr[...] = jnp.zeros(l_scr.shape, F32)
        acc_scr[...] = jnp.zeros(acc_scr.shape, F32)

    def update(s, vb):
        m_old = m_scr[...]
        m_new = jnp.maximum(m_old, jnp.max(s, axis=-1, keepdims=True))
        alpha = jnp.exp(m_old - m_new)
        pr = jnp.exp(s - m_new)
        l_scr[...] = alpha * l_scr[...] + jnp.sum(pr, axis=-1, keepdims=True)
        acc_scr[...] = alpha * acc_scr[...] + jnp.dot(pr.astype(BF16), vb, preferred_element_type=F32)
        m_scr[...] = m_new

    q2 = q2_scr[...]
    s_parts = []
    for pi in range(pps):
        kb = k_refs[pi][0].astype(BF16)
        s = lax.dot_general(q2, kb, NT_DIMS, preferred_element_type=F32)
        mask = jnp.where(last, pmask_ref[1], pmask_ref[0]) if pi == pps - 1 else pmask_ref[0]
        s_parts.append(s + mask)
    vb = jnp.concatenate([vr[0].astype(BF16) for vr in v_refs], axis=0)
    update(jnp.concatenate(s_parts, axis=1), vb)

    @pl.when(last)
    def _():
        nk = kn_ref.shape[1]
        pad = jnp.zeros((PAGE - nk, LANES), F32)
        kn = jnp.concatenate([kn_ref[0], pad], axis=0).astype(BF16)
        vn = jnp.concatenate([vn_ref[0], pad], axis=0).astype(BF16)
        sn = lax.dot_general(q2, kn, NT_DIMS, preferred_element_type=F32) + nmask_ref[...]
        update(sn, vn)
        lam = _lambda(lam_ref, lam_init)
        outs = []
        for h in range(ATT_HEADS):
            r0 = 2 * h * l
            o0 = acc_scr[r0:r0 + l, :] / l_scr[r0:r0 + l, :]
            o1 = acc_scr[r0 + l:r0 + 2 * l, :] / l_scr[r0 + l:r0 + 2 * l, :]
            outs.append(_subln(o0 - lam * o1, sg_ref, lam_init))
        o_ref[0] = jnp.concatenate(outs, axis=1).astype(BF16)


def diff_attention_decode(q, k_new, v_new, cache_k, cache_v, page_table, pmask, nmask, lam_vecs, subln_g, lam_init):
    n, l, _ = q.shape
    n_pages = page_table.shape[1]
    pps = next(c for c in (DECODE_PAGES_PER_STEP, 4, 2, 1) if n_pages % c == 0)
    ns = n_pages // pps
    rows = 2 * ATT_HEADS * l
    pk = PAGE * ATT_HEADS
    kern = functools.partial(_attn_decode_kernel, pps=pps, l=l, lam_init=lam_init)

    def page_spec(pi):
        return pl.BlockSpec((1, pk, LANES), lambda b, s, pt: (pt[b * n_pages + s * pps + pi], 0, 0))

    seq = lambda b, s, pt: (b, 0, 0)
    grid_spec = pltpu.PrefetchScalarGridSpec(
        num_scalar_prefetch=1,
        grid=(n, ns),
        in_specs=[pl.BlockSpec((4, ATT_DH), lambda b, s, pt: (0, 0)),
                  pl.BlockSpec((1, l, ATT_WIDTH), seq),
                  pl.BlockSpec((1, l * ATT_HEADS, LANES), seq),
                  pl.BlockSpec((1, l * ATT_HEADS, LANES), seq),
                  pl.BlockSpec((2, rows, pk), lambda b, s, pt: (0, 0, 0)),
                  pl.BlockSpec((rows, PAGE), lambda b, s, pt: (0, 0)),
                  pl.BlockSpec((1, ATT_DV), lambda b, s, pt: (0, 0))]
                 + [page_spec(pi) for pi in range(pps)] * 2,
        out_specs=pl.BlockSpec((1, l, ATT_WIDTH), seq),
        scratch_shapes=[pltpu.VMEM((rows, LANES), BF16), pltpu.VMEM((rows, 1), F32),
                        pltpu.VMEM((rows, 1), F32), pltpu.VMEM((rows, ATT_DV), F32)])
    return pl.pallas_call(
        kern,
        grid_spec=grid_spec,
        out_shape=jax.ShapeDtypeStruct((n, l, ATT_WIDTH), BF16),
        compiler_params=_params(("arbitrary", "arbitrary")),
        name="diff_attn_decode",
    )(page_table.reshape(-1), lam_vecs, q, k_new, v_new, pmask, nmask, subln_g.reshape(1, ATT_DV),
      *([cache_k] * pps), *([cache_v] * pps))


def decode_masks(bt, l):
    rows = 2 * ATT_HEADS * l
    r = np.arange(rows)
    row_head, row_tok = r // (2 * l), r % l
    c = np.arange(PAGE * ATT_HEADS)
    head_ok = (c[None, :] % ATT_HEADS) == row_head[:, None]
    neg = np.where(head_ok, 0.0, -np.inf).astype(np.float32)
    cn = np.arange(PAGE)
    key_n = cn // ATT_HEADS
    new_ok = ((cn[None, :] % ATT_HEADS) == row_head[:, None]) & (key_n[None, :] <= row_tok[:, None]) \
        & (key_n[None, :] < l)
    neg_n = np.where(new_ok, 0.0, -np.inf).astype(np.float32)
    per_row = jnp.broadcast_to(bt[:, :, None], (ATT_HEADS, 2, 2, l, PAGE))
    last = jnp.repeat(per_row[:, 0].reshape(rows, PAGE), ATT_HEADS, axis=1)
    new = jnp.repeat(per_row[:, 1].reshape(rows, PAGE)[:, :PAGE // ATT_HEADS], ATT_HEADS, axis=1)
    pmask = jnp.stack([jnp.asarray(neg), last + neg])
    nmask = new + neg_n
    return pmask, nmask


_PEER_CAND = [(i, j) for i in range(PEER_NTOP) for j in range(PEER_NTOP) if (i + 1) * (j + 1) <= PEER_NTOP]


def _route_kernel(hn_ref, wq_ref, sk_ref, e2_o, ec_o, e1_o, q_scr, a_scr, b_scr):
    q_scr[...] = jnp.dot(hn_ref[...], wq_ref[...], preferred_element_type=F32)
    for h in range(PEER_HEADS):
        for m in range(2):
            c0 = (2 * h + m) * LANES
            qs = q_scr[:, c0:c0 + LANES].astype(BF16)
            s = lax.dot_general(sk_ref[m], qs, NT_DIMS, preferred_element_type=F32)
            (e1_o if m == 0 else e2_o)[h] = s
            top_scr = a_scr if m == 0 else b_scr
            x = s
            for r in range(PEER_NTOP):
                mx = jnp.max(x, axis=0, keepdims=True)
                top_scr[r, h:h + 1, :] = mx
                if r + 1 < PEER_NTOP:
                    x = jnp.where(x == mx, NEG_INF, x)
    a_top = [a_scr[r] for r in range(PEER_NTOP)]
    b_top = [b_scr[r] for r in range(PEER_NTOP)]
    cands = [a_top[i] + b_top[j] for (i, j) in _PEER_CAND]
    xs = list(cands)
    ranked = []
    for r in range(PEER_NTOP):
        mx = functools.reduce(jnp.maximum, xs)
        ranked.append(mx)
        if r + 1 < PEER_NTOP:
            xs = [jnp.where(x == mx, NEG_INF, x) for x in xs]
    thr = 0.5 * (ranked[PEER_TOPK - 1] + ranked[PEER_TOPK])
    top = a_top[0] + b_top[0]
    z = functools.reduce(lambda u, w: u + w, [jnp.where(c >= thr, jnp.exp(c - top), 0.0) for c in cands])
    scale = 0.5 / z
    for h in range(PEER_HEADS):
        s1 = e1_o[h]
        s2 = e2_o[h]
        a0 = a_top[0][h:h + 1]
        b0 = b_top[0][h:h + 1]
        e2_o[h] = jnp.exp(s2 - b0)
        ec_o[h] = jnp.exp((thr[h:h + 1] - b0) - s1)
        e1_o[h] = jnp.exp(s1 - a0) * scale[h:h + 1]


def peer_route(hn, wq, sk):
    t, d = hn.shape
    tm = min(256, t)
    spec = pl.BlockSpec((PEER_HEADS, PEER_NKEYS, tm), lambda i: (0, 0, i))
    shape = jax.ShapeDtypeStruct((PEER_HEADS, PEER_NKEYS, t), F32)
    return pl.pallas_call(
        _route_kernel,
        grid=(t // tm,),
        in_specs=[pl.BlockSpec((tm, d), lambda i: (i, 0)),
                  pl.BlockSpec(wq.shape, lambda i: (0, 0)),
                  pl.BlockSpec(sk.shape, lambda i: (0, 0, 0))],
        out_specs=[spec, spec, spec],
        out_shape=[shape, shape, shape],
        scratch_shapes=[pltpu.VMEM((tm, wq.shape[1]), F32),
                        pltpu.VMEM((PEER_NTOP, PEER_HEADS, tm), F32),
                        pltpu.VMEM((PEER_NTOP, PEER_HEADS, tm), F32)],
        compiler_params=_params(("arbitrary",)),
        name="peer_route",
    )(hn, wq, sk)


GELU_C0 = math.sqrt(2.0 / math.pi)
GELU_C1 = GELU_C0 * 0.044715


def _peer_kernel(hn_ref, u_ref, vt_ref, e2_ref, ec_ref, e1_ref, x1_ref, g2_ref, o_ref, acc_scr, act_scr, wa_scr):
    e = pl.program_id(2)
    ne = pl.num_programs(2) - 1

    def weighted_acts(slot):
        hn = hn_ref[...]
        chunk = PEER_SCORE_CHUNK * PEER_NKEYS
        for cb in range(PEER_EB // chunk):
            crows = slice(cb * chunk, (cb + 1) * chunk)
            act_scr[crows, :] = lax.dot_general(u_ref[crows, :], hn, NT_DIMS, preferred_element_type=F32)
            for il in range(cb * PEER_SCORE_CHUNK, (cb + 1) * PEER_SCORE_CHUNK):
                rows = slice(il * PEER_NKEYS, (il + 1) * PEER_NKEYS)
                w = None
                for h in range(PEER_HEADS):
                    e2 = e2_ref[h]
                    term = jnp.where(e2 >= ec_ref[h, il:il + 1, :], e2 * e1_ref[h, il:il + 1, :], 0.0)
                    w = term if w is None else w + term
                x = act_scr[rows, :]
                inner = x * (GELU_C0 + GELU_C1 * (x * x))
                wa_scr[slot, rows, :] = (w * x * (1.0 + jnp.tanh(inner))).astype(BF16)

    def values(slot):
        acc_scr[...] += jnp.dot(vt_ref[...], wa_scr[slot], preferred_element_type=F32)

    @pl.when(e == 0)
    def _():
        acc_scr[...] = jnp.zeros(acc_scr.shape, F32)
        weighted_acts(0)

    for par in range(2):
        @pl.when((e >= 1) & (e < ne) & (e % 2 == par))
        def _():
            weighted_acts(par)
            values(1 - par)

    @pl.when(e == ne)
    def _():
        values(1)
        y = acc_scr[...].T
        o_ref[...] = x1_ref[...] + g2_ref[...] * y.reshape(x1_ref.shape)


def peer_mix(hn, u_b, vt_b, e2t, ect, e1t, x1, g2):
    n, l, d = x1.shape
    nb, lb = _row_blocks(n, l)
    tm = nb * lb
    lk = l // lb
    ne = PEER_EXPERTS // PEER_EB
    assert ne % 2 == 0
    tok = lambda i, k, e: (i * lk + k, 0)
    cur = lambda e: jnp.minimum(e, ne - 1)
    prev = lambda e: jnp.maximum(e - 1, 0)
    return pl.pallas_call(
        _peer_kernel,
        grid=(n // nb, lk, ne + 1),
        in_specs=[pl.BlockSpec((tm, d), tok),
                  pl.BlockSpec((PEER_EB, d), lambda i, k, e: (cur(e), 0)),
                  pl.BlockSpec((d, PEER_EB), lambda i, k, e: (0, prev(e))),
                  pl.BlockSpec((PEER_HEADS, PEER_NKEYS, tm), lambda i, k, e: (0, 0, i * lk + k)),
                  pl.BlockSpec((PEER_HEADS, PEER_IB, tm), lambda i, k, e: (0, cur(e), i * lk + k)),
                  pl.BlockSpec((PEER_HEADS, PEER_IB, tm), lambda i, k, e: (0, cur(e), i * lk + k)),
                  pl.BlockSpec((nb, lb, d), lambda i, k, e: (i, k, 0)),
                  pl.BlockSpec((nb, 1, d), lambda i, k, e: (i, 0, 0))],
        out_specs=pl.BlockSpec((nb, lb, d), lambda i, k, e: (i, k, 0)),
        out_shape=jax.ShapeDtypeStruct((n, l, d), F32),
        scratch_shapes=[pltpu.VMEM((d, tm), F32), pltpu.VMEM((PEER_EB, tm), F32),
                        pltpu.VMEM((2, PEER_EB, tm), BF16)],
        compiler_params=_params(("arbitrary", "arbitrary", "arbitrary")),
        name="peer_mix",
    )(hn, u_b, vt_b, e2t, ect, e1t, x1, g2)


def peer_ffn_residual(hn, x1, g2, wq_b, sk_b, u_b, vt_b):
    e2t, ect, e1t = peer_route(hn, wq_b, sk_b)
    return peer_mix(hn, u_b, vt_b, e2t, ect, e1t, x1, g2)


def _trunk(x, mods, pos0, conv_prev, ssm_prev, pool_prev, paged_kv, p, w):
    n, l, d = x.shape
    t = n * l
    sh1, sc1, g1, sh2, sc2, g2 = mods[0]
    proj = norm_inproj(x, p["norm_mix_g"][0], sc1, sh1, w["hyb_in"], tn=HYB_PROJ // 3)
    mixed, conv_new, ssm_new, pool_new = ssd_pool(proj, n, l, conv_prev, ssm_prev, pool_prev, pos0, w["ssm"])
    x1, hn = outproj_residual(mixed, w["hyb_out"], x, g1, p["norm_ffn_g"][0], sc2, sh2)
    x2 = peer_ffn_residual(hn, x1, g2, w["peer_wq"][0], w["peer_sk"][0], w["peer_u"][0], w["peer_vt"][0])

    sh1, sc1, g1, sh2, sc2, g2 = mods[1]
    qb, kf, kb, vf, vt = norm_qkv(x2, p["norm_mix_g"][1], sc1, sh1, w["att_in"], p["att_q_g"][0], p["att_k_g"][0])
    lam_init = 0.8 - 0.6 * math.exp(-0.3 * 1)
    lam_vecs = jnp.stack([p["att_lambda_q1"][0], p["att_lambda_k1"][0],
                          p["att_lambda_q2"][0], p["att_lambda_k2"][0]])
    if paged_kv is None:
        tq = min(ATT_TQ, l)
        bias_t = bias_tiles(p["rel_bias"], (("zero", 0), ("bias", tq), ("bias_causal", 0), ("neg", 0)),
                            tq, tq, sign=-1)
        o = diff_attention_prompt(qb, kb, vt, n, l, bias_t, lam_vecs, p["att_subln_g"][0], lam_init, tq)
    else:
        cache_k, cache_v, page_table = paged_kv
        past_len = page_table.shape[1] * PAGE
        bt = bias_tiles(p["rel_bias"], (("bias", pos0 - (past_len - PAGE)), ("bias", pos0 - past_len)),
                        l, PAGE)
        pmask, nmask = decode_masks(bt, l)
        o = diff_attention_decode(qb.reshape(n, l, ATT_WIDTH), kf.reshape(n, l * ATT_HEADS, ATT_DV),
                                  vf.reshape(n, l * ATT_HEADS, ATT_DV), cache_k, cache_v, page_table,
                                  pmask, nmask, lam_vecs, p["att_subln_g"][0], lam_init).reshape(t, ATT_WIDTH)
    x3, hn = outproj_residual(o, w["att_out"], x2, g1, p["norm_ffn_g"][1], sc2, sh2)
    x4 = peer_ffn_residual(hn, x3, g2, w["peer_wq"][1], w["peer_sk"][1], w["peer_u"][1], w["peer_vt"][1])
    k_new = kf.reshape(1, n, l, ATT_HEADS, 2 * ATT_DH)
    v_new = vf.reshape(1, n, l, ATT_HEADS, ATT_DV)
    return x4, conv_new[None], ssm_new[None], pool_new[None], k_new, v_new


def kernel(x_prompt, x_sample, state_conv, state_ssm, state_pool, cache_k, cache_v, page_table,
           c_prompt, c_sample, rel_bias, norm_mix_g, norm_ffn_g, ada_w, ada_b,
           hyb_w_in, ssm_conv_w, ssm_conv_b, ssm_dt_bias, ssm_a_log, ssm_d, ssm_norm_g,
           pool_w, pool_scale, hyb_w_out, att_w_in, att_q_g, att_k_g,
           att_lambda_q1, att_lambda_k1, att_lambda_q2, att_lambda_k2, att_subln_g, att_w_out,
           peer_wq, peer_subkeys, peer_u, peer_v):
    p = {
        "rel_bias": rel_bias, "norm_mix_g": norm_mix_g, "norm_ffn_g": norm_ffn_g,
        "att_q_g": att_q_g, "att_k_g": att_k_g,
        "att_lambda_q1": att_lambda_q1, "att_lambda_k1": att_lambda_k1,
        "att_lambda_q2": att_lambda_q2, "att_lambda_k2": att_lambda_k2, "att_subln_g": att_subln_g,
    }
    w_in = hyb_w_in[0]
    z_w = w_in[:, :SSM_D_INNER]
    xbc_w = w_in[:, SSM_D_INNER:SSM_D_INNER + SSM_CONV_DIM]
    dt_w = w_in[:, SSM_D_INNER + SSM_CONV_DIM:SSM_D_INNER + SSM_CONV_DIM + SSM_HEADS]
    u_w = w_in[:, SSM_D_INNER + SSM_CONV_DIM + SSM_HEADS:]
    hyb_in = jnp.concatenate([z_w, xbc_w, u_w, jnp.pad(dt_w, ((0, 0), (0, LANES - SSM_HEADS)))], axis=1)
    w = {
        "hyb_in": hyb_in.astype(BF16),
        "hyb_out": hyb_w_out[0].astype(BF16),
        "att_in": att_w_in[0].astype(BF16),
        "att_out": att_w_out[0].astype(BF16),
        "peer_wq": peer_wq.astype(BF16),
        "peer_sk": peer_subkeys.astype(BF16),
        "peer_u": peer_u.astype(BF16),
        "peer_vt": jnp.swapaxes(peer_v, 1, 2).astype(BF16),
        "ssm": {"ssm_conv_w": ssm_conv_w[0], "ssm_conv_b": ssm_conv_b[0], "ssm_dt_bias": ssm_dt_bias[0],
                "ssm_a_log": ssm_a_log[0], "ssm_d": ssm_d[0], "ssm_norm_g": ssm_norm_g[0],
                "pool_w": pool_w[0], "pool_scale": pool_scale[0]},
    }
    nb, ns = x_prompt.shape[0], x_sample.shape[0]
    d = x_prompt.shape[2]
    rows = nb + ns
    rpad = -rows % SUBLANES
    c_all = jnp.concatenate([c_prompt, c_sample, jnp.zeros((rpad, d), F32)], axis=0)
    mod = ada_mod(c_all, ada_w, ada_b)

    def mods_for(lo, cnt):
        return [tuple(mod[i, lo:lo + cnt, k * d:(k + 1) * d].reshape(cnt, 1, d) for k in range(6))
                for i in range(mod.shape[0])]

    conv0 = jnp.zeros((nb, SSM_CONV - 1, SSM_CONV_DIM), F32)
    ssm0 = jnp.zeros((nb, SSM_HEADS, SSM_HEAD_DIM, SSM_D_STATE), F32)
    pool0 = jnp.zeros((nb, POOL_HIST, POOL_DIM), F32)
    yp, conv_p, ssm_p, pool_p, k_p, v_p = _trunk(x_prompt, mods_for(0, nb), 0, conv0, ssm0, pool0, None, p, w)
    past_len = page_table.shape[1] * cache_k.shape[2]
    n_phys = cache_k.shape[1]
    ys, conv_s, ssm_s, pool_s, k_s, v_s = _trunk(
        x_sample, mods_for(nb, ns), past_len, state_conv[0], state_ssm[0], state_pool[0],
        (cache_k[0].reshape(n_phys, PAGE * ATT_HEADS, ATT_DV), cache_v[0].reshape(n_phys, PAGE * ATT_HEADS, ATT_DV),
         page_table),
        p, w)
    return (yp, ys, conv_p, ssm_p, pool_p, k_p, v_p, conv_s, ssm_s, pool_s, k_s, v_s)
```

```python
import functools
import math

import numpy as np
import jax
import jax.numpy as jnp
from jax import lax
from jax.experimental import pallas as pl
from jax.experimental.pallas import tpu as pltpu

F32 = jnp.float32
BF16 = jnp.bfloat16

D_MODEL = 1024
EPS = 1e-6
LANES = 128
SUBLANES = 8

SSM_D_INNER = 2048
SSM_HEAD_DIM = 64
SSM_HEADS = 32
SSM_GROUPS = 2
SSM_D_STATE = 128
SSM_CONV = 4
SSM_CHUNK = 128
SSM_CONV_DIM = 2560
GROUP_W = SSM_D_INNER // SSM_GROUPS

POOL_WINDOWS = (2, 4, 8, 16)
POOL_DIM = 1024
POOL_GROUP = 256
POOL_HIST = 15
POOL_HALO = 16
CONV_HALO = 8

HYB_PROJ = SSM_D_INNER + SSM_CONV_DIM + POOL_DIM + LANES
HYB_MIX = SSM_D_INNER + POOL_DIM

ATT_HEADS = 8
ATT_DH = 64
ATT_DV = 128
ATT_WIDTH = 1024
REL_BUCKETS = 32
REL_MAX_DIST = 128
PAGE = 128
ATT_TQ = 512
DECODE_PAGES_PER_STEP = 8

PEER_HEADS = 8
PEER_NKEYS = 128
PEER_EXPERTS = PEER_NKEYS * PEER_NKEYS
PEER_TOPK = 16
PEER_NTOP = PEER_TOPK + 1
PEER_EB = 1024
PEER_IB = PEER_EB // PEER_NKEYS

NEG_INF = float("-inf")
NT_DIMS = (((1,), (1,)), ((), ()))

VMEM_LIMIT = 56 * 1024 * 1024


def _params(sem):
    return pltpu.CompilerParams(dimension_semantics=sem, vmem_limit_bytes=VMEM_LIMIT)


def _sigmoid(x):
    return 1.0 / (1.0 + jnp.exp(-x))


def _silu(x):
    return x * _sigmoid(x)


def _split3(x):
    hi = x.astype(BF16)
    r1 = x - hi.astype(F32)
    mid = r1.astype(BF16)
    lo = (r1 - mid.astype(F32)).astype(BF16)
    return hi, mid, lo


def _norm_mod(x3, g_ref, sc_ref, sh_ref):
    ms = jnp.mean(x3 * x3, axis=-1, keepdims=True)
    y = x3 * lax.rsqrt(ms + EPS) * g_ref[...]
    return y * (1.0 + sc_ref[...]) + sh_ref[...]


def _row_blocks(n, l):
    if l >= 512:
        return 1, 512
    if l >= SUBLANES and l % SUBLANES == 0 and l < 512:
        if l == SUBLANES:
            nb = min(n, 64)
            while n % nb:
                nb //= 2
            return nb, l
        return 1, l
    raise ValueError(f"unsupported sequence length {l}")


def _ada_kernel(c_ref, w_ref, b_ref, o_ref):
    ca = _silu(c_ref[...]).astype(BF16)
    o_ref[0] = jnp.dot(ca, w_ref[0].astype(BF16), preferred_element_type=F32) + b_ref[0]


def ada_mod(c_all, ada_w, ada_b):
    r = c_all.shape[0]
    depth, d, n6 = ada_w.shape
    tn = 512
    return pl.pallas_call(
        _ada_kernel,
        grid=(depth, n6 // tn),
        in_specs=[pl.BlockSpec((r, d), lambda i, j: (0, 0)),
                  pl.BlockSpec((1, d, tn), lambda i, j: (i, 0, j)),
                  pl.BlockSpec((1, 1, tn), lambda i, j: (i, 0, j))],
        out_specs=pl.BlockSpec((1, r, tn), lambda i, j: (i, 0, j)),
        out_shape=jax.ShapeDtypeStruct((depth, r, n6), F32),
        compiler_params=_params(("arbitrary", "arbitrary")),
        name="ada_mod",
    )(c_all, ada_w, ada_b.reshape(depth, 1, n6))


def _inproj_kernel(x_ref, g_ref, sc_ref, sh_ref, w_ref, o_ref, h_scr):
    @pl.when(pl.program_id(2) == 0)
    def _():
        h = _norm_mod(x_ref[...], g_ref, sc_ref, sh_ref)
        h_scr[...] = h.reshape(h_scr.shape).astype(BF16)

    o_ref[...] = jnp.dot(h_scr[...], w_ref[...], preferred_element_type=F32)


def norm_inproj(x, g, sc, sh, w, tn):
    n, l, d = x.shape
    nb, lb = _row_blocks(n, l)
    m = nb * lb
    lk = l // lb
    nout = w.shape[1]
    return pl.pallas_call(
        _inproj_kernel,
        grid=(n // nb, lk, nout // tn),
        in_specs=[pl.BlockSpec((nb, lb, d), lambda i, k, j: (i, k, 0)),
                  pl.BlockSpec((1, d), lambda i, k, j: (0, 0)),
                  pl.BlockSpec((nb, 1, d), lambda i, k, j: (i, 0, 0)),
                  pl.BlockSpec((nb, 1, d), lambda i, k, j: (i, 0, 0)),
                  pl.BlockSpec((d, tn), lambda i, k, j: (0, j))],
        out_specs=pl.BlockSpec((m, tn), lambda i, k, j: (i * lk + k, j)),
        out_shape=jax.ShapeDtypeStruct((n * l, nout), F32),
        scratch_shapes=[pltpu.VMEM((m, d), BF16)],
        compiler_params=_params(("arbitrary", "arbitrary", "arbitrary")),
        name="norm_inproj",
    )(x, g.reshape(1, d), sc, sh, w)


def _seg_rms(y, g128, bd):
    sq = y * y
    hi = sq.astype(BF16)
    lo = (sq - hi.astype(F32)).astype(BF16)
    outs = []
    for cb in range(y.shape[1] // LANES):
        sl = slice(cb * LANES, (cb + 1) * LANES)
        ms = (jnp.dot(hi[:, sl], bd, preferred_element_type=F32)
              + jnp.dot(lo[:, sl], bd, preferred_element_type=F32))
        outs.append(y[:, sl] * lax.rsqrt(ms + EPS) * g128)
    return jnp.concatenate(outs, axis=1)


def _qkv_kernel(x_ref, g_ref, sc_ref, sh_ref, w_ref, qg_ref, kg_ref, bd_ref,
                q_o, kf_o, kb_o, vf_o, vt_o, h_scr):
    j = pl.program_id(2)

    @pl.when(j == 0)
    def _():
        h = _norm_mod(x_ref[...], g_ref, sc_ref, sh_ref)
        h_scr[...] = h.reshape(h_scr.shape).astype(BF16)

    y = jnp.dot(h_scr[...], w_ref[...], preferred_element_type=F32)

    @pl.when(j == 0)
    def _():
        q_o[...] = (_seg_rms(y, qg_ref[...], bd_ref[...]) * (ATT_DH ** -0.5)).astype(BF16)

    @pl.when(j == 1)
    def _():
        kn = _seg_rms(y, kg_ref[...], bd_ref[...])
        kf_o[...] = kn
        kb_o[...] = kn.astype(BF16)

    @pl.when(j == 2)
    def _():
        vf_o[...] = y
        vt_o[...] = y.T.astype(BF16)


def norm_qkv(x, g, sc, sh, w, q_g, k_g):
    n, l, d = x.shape
    nb, lb = _row_blocks(n, l)
    m = nb * lb
    lk = l // lb
    t = n * l
    seg = np.arange(LANES) // ATT_DH
    bd = jnp.asarray((seg[:, None] == seg[None, :]).astype(np.float32) / ATT_DH, BF16)
    row = lambda i, k, j: (i * lk + k, 0)
    const2 = lambda i, k, j: (0, 0)
    outs = pl.pallas_call(
        _qkv_kernel,
        grid=(n // nb, lk, 3),
        in_specs=[pl.BlockSpec((nb, lb, d), lambda i, k, j: (i, k, 0)),
                  pl.BlockSpec((1, d), const2),
                  pl.BlockSpec((nb, 1, d), lambda i, k, j: (i, 0, 0)),
                  pl.BlockSpec((nb, 1, d), lambda i, k, j: (i, 0, 0)),
                  pl.BlockSpec((d, ATT_WIDTH), lambda i, k, j: (0, j)),
                  pl.BlockSpec((1, LANES), const2),
                  pl.BlockSpec((1, LANES), const2),
                  pl.BlockSpec((LANES, LANES), const2)],
        out_specs=[pl.BlockSpec((m, ATT_WIDTH), row)] * 4
                  + [pl.BlockSpec((ATT_WIDTH, m), lambda i, k, j: (0, i * lk + k))],
        out_shape=[jax.ShapeDtypeStruct((t, ATT_WIDTH), BF16),
                   jax.ShapeDtypeStruct((t, ATT_WIDTH), F32),
                   jax.ShapeDtypeStruct((t, ATT_WIDTH), BF16),
                   jax.ShapeDtypeStruct((t, ATT_WIDTH), F32),
                   jax.ShapeDtypeStruct((ATT_WIDTH, t), BF16)],
        scratch_shapes=[pltpu.VMEM((m, d), BF16)],
        compiler_params=_params(("arbitrary", "arbitrary", "arbitrary")),
        name="norm_qkv",
    )(x, g.reshape(1, d), sc, sh, w,
      jnp.tile(q_g, 2).reshape(1, LANES), jnp.tile(k_g, 2).reshape(1, LANES), bd)
    return outs


def _outproj_kernel(a_ref, w_ref, x_ref, g1_ref, gn_ref, sc_ref, sh_ref, x1_o, hn_o):
    y = jnp.dot(a_ref[...], w_ref[...], preferred_element_type=F32)
    x1 = x_ref[...] + g1_ref[...] * y.reshape(x_ref.shape)
    x1_o[...] = x1
    hn_o[...] = _norm_mod(x1, gn_ref, sc_ref, sh_ref).reshape(hn_o.shape).astype(BF16)


def outproj_residual(a, w, x, g1, gn, sc, sh):
    n, l, d = x.shape
    nb, lb = _row_blocks(n, l)
    m = nb * lb
    lk = l // lb
    kdim = a.shape[1]
    seq = lambda i, k: (i, 0, 0)
    return pl.pallas_call(
        _outproj_kernel,
        grid=(n // nb, lk),
        in_specs=[pl.BlockSpec((m, kdim), lambda i, k: (i * lk + k, 0)),
                  pl.BlockSpec((kdim, d), lambda i, k: (0, 0)),
                  pl.BlockSpec((nb, lb, d), lambda i, k: (i, k, 0)),
                  pl.BlockSpec((nb, 1, d), seq),
                  pl.BlockSpec((1, d), lambda i, k: (0, 0)),
                  pl.BlockSpec((nb, 1, d), seq),
                  pl.BlockSpec((nb, 1, d), seq)],
        out_specs=[pl.BlockSpec((nb, lb, d), lambda i, k: (i, k, 0)),
                   pl.BlockSpec((m, d), lambda i, k: (i * lk + k, 0))],
        out_shape=[jax.ShapeDtypeStruct((n, l, d), F32),
                   jax.ShapeDtypeStruct((n * l, d), BF16)],
        compiler_params=_params(("arbitrary", "arbitrary")),
        name="outproj_residual",
    )(a, w, x, g1, gn.reshape(1, d), sc, sh)


def _ssd_pool_kernel(proj_ref, convp_ref, ssmp_ref, poolp_ref, convw_ref, convb_ref, dtb_ref, alog_ref,
                     dexp_ref, normg_ref, poolw_ref, pools_ref, tril_ref, rexp_ref,
                     mixed_o, convn_o, ssmn_o, pooln_o,
                     ext_c, ext_p, ht_scr, *, lr, pos0):
    q = SSM_CHUNK
    c = pl.program_id(1)
    nc = pl.num_programs(1)

    @pl.when(c == 0)
    def _():
        ext_c[0:CONV_HALO, :] = convp_ref[0]
        ext_p[0:POOL_HALO, :] = poolp_ref[0]
        for g in range(SSM_GROUPS):
            ht_scr[g] = ssmp_ref[0, g * GROUP_W:(g + 1) * GROUP_W, :].T

    xbc_cols = slice(SSM_D_INNER, SSM_D_INNER + SSM_CONV_DIM)
    u_cols = slice(SSM_D_INNER + SSM_CONV_DIM, SSM_D_INNER + SSM_CONV_DIM + POOL_DIM)
    dt_cols = slice(SSM_D_INNER + SSM_CONV_DIM + POOL_DIM, HYB_PROJ)

    ext_c[CONV_HALO:CONV_HALO + lr, :] = proj_ref[:, xbc_cols]
    ext_p[POOL_HALO:POOL_HALO + lr, :] = proj_ref[:, u_cols]
    if lr < q:
        ext_c[CONV_HALO + lr:CONV_HALO + q, :] = jnp.zeros((q - lr, SSM_CONV_DIM), F32)
        ext_p[POOL_HALO + lr:POOL_HALO + q, :] = jnp.zeros((q - lr, POOL_DIM), F32)

    acc = convb_ref[...] + ext_c[pl.ds(CONV_HALO - 3, q), :] * convw_ref[0:1, :]
    for tap in range(1, SSM_CONV):
        acc = acc + ext_c[pl.ds(CONV_HALO - 3 + tap, q), :] * convw_ref[tap:tap + 1, :]
    xbc = _silu(acc)
    xs = xbc[:, :SSM_D_INNER]

    row = lax.broadcasted_iota(jnp.int32, (q, LANES), 0)
    dt_raw = proj_ref[:, dt_cols] + dtb_ref[...]
    if lr < q:
        dt_raw = jnp.concatenate([dt_raw, jnp.zeros((q - lr, LANES), F32)], axis=0)
    dt = jnp.maximum(dt_raw, 0.0) + jnp.log(1.0 + jnp.exp(-jnp.abs(dt_raw)))
    dt = jnp.where(row < lr, dt, 0.0)
    a_neg = -jnp.exp(alog_ref[...])
    d_a = dt * a_neg

    tril = tril_ref[...]
    rexp = rexp_ref[...]
    a_cs = sum(jnp.dot(tril, p, preferred_element_type=F32) for p in _split3(d_a))
    acs_e = sum(jnp.dot(p, rexp, preferred_element_type=F32) for p in _split3(a_cs))
    dt_e = sum(jnp.dot(p, rexp, preferred_element_type=F32) for p in _split3(dt))
    a_cs_t = a_cs.T

    xdt = xs * dt_e
    alast_e = acs_e[lr - 1:lr, :]
    xdtw = (xdt * jnp.exp(alast_e - acs_e)).astype(BF16)
    ea_e = jnp.exp(acs_e)
    xdt_b = xdt.astype(BF16)

    ii = lax.broadcasted_iota(jnp.int32, (q, q), 0)
    jj = lax.broadcasted_iota(jnp.int32, (q, q), 1)
    causal = ii >= jj
    lane = lax.broadcasted_iota(jnp.int32, (q, LANES), 1)
    even_half = lane < SSM_HEAD_DIM
    zero_b = jnp.zeros((q, LANES), BF16)

    ys = []
    for g in range(SSM_GROUPS):
        bg = xbc[:, SSM_D_INNER + g * SSM_D_STATE:SSM_D_INNER + (g + 1) * SSM_D_STATE]
        cg = xbc[:, SSM_D_INNER + SSM_GROUPS * SSM_D_STATE + g * SSM_D_STATE:
                 SSM_D_INNER + SSM_GROUPS * SSM_D_STATE + (g + 1) * SSM_D_STATE].astype(BF16)
        bg_t = bg.T.astype(BF16)
        gcols = slice(g * GROUP_W, (g + 1) * GROUP_W)
        cb = jnp.dot(cg, bg_t, preferred_element_type=F32)
        h_in = ht_scr[g]
        y_off = jnp.dot(cg, h_in.astype(BF16), preferred_element_type=F32) * ea_e[:, gcols]
        st = jnp.dot(bg_t, xdtw[:, gcols], preferred_element_type=F32)
        ht_scr[g] = h_in * jnp.exp(alast_e[:, gcols]) + st
        pieces = []
        for pr in range(GROUP_W // LANES):
            ms = []
            for hh in range(2):
                h = g * (SSM_HEADS // SSM_GROUPS) + 2 * pr + hh
                seg = a_cs[:, h:h + 1] - a_cs_t[h:h + 1, :]
                ms.append((cb * jnp.exp(jnp.where(causal, seg, NEG_INF))).astype(BF16))
            xblk = xdt_b[:, g * GROUP_W + pr * LANES:g * GROUP_W + (pr + 1) * LANES]
            rhs = jnp.concatenate([jnp.where(even_half, xblk, zero_b),
                                   jnp.where(even_half, zero_b, xblk)], axis=0)
            pieces.append(jnp.dot(jnp.concatenate(ms, axis=1), rhs, preferred_element_type=F32))
        ys.append(jnp.concatenate(pieces, axis=1) + y_off)
    y = jnp.concatenate(ys, axis=1) + dexp_ref[...] * xs
    y = y * _silu(proj_ref[:, :SSM_D_INNER]) if lr == q else (
        y * _silu(jnp.concatenate([proj_ref[:, :SSM_D_INNER], jnp.zeros((q - lr, SSM_D_INNER), F32)], axis=0)))
    for g in range(SSM_GROUPS):
        gcols = slice(g * GROUP_W, (g + 1) * GROUP_W)
        yg = y[:, gcols]
        ms = jnp.mean(yg * yg, axis=-1, keepdims=True)
        yn = yg * lax.rsqrt(ms + EPS) * normg_ref[:, gcols]
        mixed_o[:, gcols] = yn[:lr].astype(BF16)

    pos = pos0 + c * q + lax.broadcasted_iota(jnp.int32, (q, POOL_GROUP), 0)
    for gi, w in enumerate(POOL_WINDOWS):
        cols = slice(gi * POOL_GROUP, (gi + 1) * POOL_GROUP)
        cur = ext_p[pl.ds(POOL_HALO, q), cols]
        wsum = cur
        for k in range(1, w):
            wsum = wsum + ext_p[pl.ds(POOL_HALO - k, q), cols]
        cnt = jnp.minimum(w, pos + 1).astype(F32)
        pooled = wsum / cnt - cur
        yp = jnp.dot(pooled.astype(BF16), poolw_ref[gi], preferred_element_type=F32) * pools_ref[:, cols]
        mixed_o[:, SSM_D_INNER + gi * POOL_GROUP:SSM_D_INNER + (gi + 1) * POOL_GROUP] = yp[:lr].astype(BF16)

    new_c = ext_c[lr:lr + CONV_HALO, :]
    new_p = ext_p[lr:lr + POOL_HALO, :]
    ext_c[0:CONV_HALO, :] = new_c
    ext_p[0:POOL_HALO, :] = new_p

    @pl.when(c == nc - 1)
    def _():
        convn_o[0] = new_c
        pooln_o[0] = new_p
        for g in range(SSM_GROUPS):
            ssmn_o[0, g * GROUP_W:(g + 1) * GROUP_W, :] = ht_scr[g].T


def ssd_pool(proj, n, l, conv_prev, ssm_prev, pool_prev, pos0, p):
    q = SSM_CHUNK
    if l % q == 0:
        lr, nc = q, l // q
    elif l < q and l % SUBLANES == 0:
        lr, nc = l, 1
    else:
        raise ValueError(f"unsupported sequence length {l}")
    convp = jnp.pad(conv_prev, ((0, 0), (CONV_HALO - (SSM_CONV - 1), 0), (0, 0)))
    poolp = jnp.pad(pool_prev, ((0, 0), (POOL_HALO - POOL_HIST, 0), (0, 0)))
    ssmp = ssm_prev.reshape(n, SSM_D_INNER, SSM_D_STATE)
    pad_h = LANES - SSM_HEADS
    tril = jnp.asarray(np.tril(np.ones((q, q), np.float32)), BF16)
    rexp_np = np.zeros((LANES, SSM_D_INNER), np.float32)
    rexp_np[np.arange(SSM_D_INNER) // SSM_HEAD_DIM, np.arange(SSM_D_INNER)] = 1.0
    rexp = jnp.asarray(rexp_np, BF16)
    c2 = lambda b, c: (0, 0)
    seq3 = lambda b, c: (b, 0, 0)
    kern = functools.partial(_ssd_pool_kernel, lr=lr, pos0=pos0)
    mixed, convn, ssmn, pooln = pl.pallas_call(
        kern,
        grid=(n, nc),
        in_specs=[pl.BlockSpec((lr, HYB_PROJ), lambda b, c: (b * nc + c, 0)),
                  pl.BlockSpec((1, CONV_HALO, SSM_CONV_DIM), seq3),
                  pl.BlockSpec((1, SSM_D_INNER, SSM_D_STATE), seq3),
                  pl.BlockSpec((1, POOL_HALO, POOL_DIM), seq3),
                  pl.BlockSpec((SSM_CONV, SSM_CONV_DIM), c2),
                  pl.BlockSpec((1, SSM_CONV_DIM), c2),
                  pl.BlockSpec((1, LANES), c2),
                  pl.BlockSpec((1, LANES), c2),
                  pl.BlockSpec((1, SSM_D_INNER), c2),
                  pl.BlockSpec((1, SSM_D_INNER), c2),
                  pl.BlockSpec((len(POOL_WINDOWS), POOL_GROUP, POOL_GROUP), lambda b, c: (0, 0, 0)),
                  pl.BlockSpec((1, POOL_DIM), c2),
                  pl.BlockSpec((q, q), c2),
                  pl.BlockSpec((LANES, SSM_D_INNER), c2)],
        out_specs=[pl.BlockSpec((lr, HYB_MIX), lambda b, c: (b * nc + c, 0)),
                   pl.BlockSpec((1, CONV_HALO, SSM_CONV_DIM), seq3),
                   pl.BlockSpec((1, SSM_D_INNER, SSM_D_STATE), seq3),
                   pl.BlockSpec((1, POOL_HALO, POOL_DIM), seq3)],
        out_shape=[jax.ShapeDtypeStruct((n * l, HYB_MIX), BF16),
                   jax.ShapeDtypeStruct((n, CONV_HALO, SSM_CONV_DIM), F32),
                   jax.ShapeDtypeStruct((n, SSM_D_INNER, SSM_D_STATE), F32),
                   jax.ShapeDtypeStruct((n, POOL_HALO, POOL_DIM), F32)],
        scratch_shapes=[pltpu.VMEM((CONV_HALO + q, SSM_CONV_DIM), F32),
                        pltpu.VMEM((POOL_HALO + q, POOL_DIM), F32),
                        pltpu.VMEM((SSM_GROUPS, SSM_D_STATE, GROUP_W), F32)],
        compiler_params=_params(("arbitrary", "arbitrary")),
        name="ssd_pool",
    )(proj, convp, ssmp, poolp,
      p["ssm_conv_w"], p["ssm_conv_b"].reshape(1, -1),
      jnp.pad(p["ssm_dt_bias"], (0, pad_h)).reshape(1, LANES),
      jnp.pad(p["ssm_a_log"], (0, pad_h)).reshape(1, LANES),
      jnp.repeat(p["ssm_d"], SSM_HEAD_DIM).reshape(1, SSM_D_INNER),
      p["ssm_norm_g"].reshape(1, SSM_D_INNER),
      p["pool_w"].astype(BF16), p["pool_scale"].reshape(1, POOL_DIM), tril, rexp)
    conv_new = convn[:, CONV_HALO - (SSM_CONV - 1):, :]
    ssm_new = ssmn.reshape(n, SSM_HEADS, SSM_HEAD_DIM, SSM_D_STATE)
    pool_new = pooln[:, POOL_HALO - POOL_HIST:, :]
    return mixed, conv_new, ssm_new, pool_new


def _bucket_bounds():
    max_exact = REL_BUCKETS // 2
    bounds = []
    for b in range(max_exact + 1, REL_BUCKETS):
        d = max_exact
        while True:
            val = math.log(d / max_exact) / math.log(REL_MAX_DIST / max_exact) * (REL_BUCKETS - max_exact)
            if max_exact + int(val) >= b:
                break
            d += 1
        bounds.append(d)
    return bounds


def _bias_kernel(rb_ref, o_ref, *, tiles, rows, cols, sign):
    h = pl.program_id(0)
    max_exact = REL_BUCKETS // 2
    bounds = _bucket_bounds()
    far = rb_ref[REL_BUCKETS - 1, h]
    ii = lax.broadcasted_iota(jnp.int32, (rows, cols), 0)
    jj = lax.broadcasted_iota(jnp.int32, (rows, cols), 1)
    for oi, (kind, off) in enumerate(tiles):
        if kind == "zero":
            o_ref[0, oi] = jnp.zeros((rows, cols), F32)
            continue
        if kind == "neg":
            o_ref[0, oi] = jnp.full((rows, cols), NEG_INF, F32)
            continue
        rel = off + sign * (ii - jj)
        dist = jnp.maximum(rel, 0)
        large = jnp.full((rows, cols), max_exact, jnp.int32)
        for bnd in bounds:
            large = large + (dist >= bnd).astype(jnp.int32)
        bucket = jnp.where(dist < max_exact, dist, large)
        bias = jnp.zeros((rows, cols), F32)
        for b in range(REL_BUCKETS):
            bias = bias + jnp.where(bucket == b, rb_ref[b, h] - far, 0.0)
        if kind == "bias_causal":
            bias = jnp.where(rel >= 0, bias, NEG_INF)
        o_ref[0, oi] = bias


def bias_tiles(rel_bias, tiles, rows, cols, sign=1):
    kern = functools.partial(_bias_kernel, tiles=tuple(tiles), rows=rows, cols=cols, sign=sign)
    return pl.pallas_call(
        kern,
        grid=(ATT_HEADS,),
        in_specs=[pl.BlockSpec(memory_space=pltpu.SMEM)],
        out_specs=pl.BlockSpec((1, len(tiles), rows, cols), lambda h: (h, 0, 0, 0)),
        out_shape=jax.ShapeDtypeStruct((ATT_HEADS, len(tiles), rows, cols), F32),
        compiler_params=_params(("arbitrary",)),
        name="bias_tiles",
    )(rel_bias)


def _lambda(lam_ref, lam_init):
    l1 = jnp.sum(lam_ref[0:1, :] * lam_ref[1:2, :], axis=-1, keepdims=True)
    l2 = jnp.sum(lam_ref[2:3, :] * lam_ref[3:4, :], axis=-1, keepdims=True)
    return jnp.exp(l1) - jnp.exp(l2) + lam_init


def _subln(o, sg_ref, lam_init):
    ms = jnp.mean(o * o, axis=-1, keepdims=True)
    return o * lax.rsqrt(ms + EPS) * sg_ref[...] * (1.0 - lam_init)


def _attn_kernel(lam_ref, q_ref, k_ref, vt_ref, bias_ref, sgt_ref, o_ref,
                 m_scr, l_scr, acc_scr, s_a, s_b, p_a, p_b, al_a, al_b, *, tq, lam_init):
    qi = pl.program_id(2)
    q = q_ref[...]
    lane = lax.broadcasted_iota(jnp.int32, (tq, LANES), 1)
    zero = jnp.zeros_like(q)
    q_halves = (jnp.where(lane < ATT_DH, q, zero), jnp.where(lane < ATT_DH, zero, q))
    m_scr[...] = jnp.full(m_scr.shape, NEG_INF, F32)
    l_scr[...] = jnp.zeros(l_scr.shape, F32)
    acc_scr[...] = jnp.zeros(acc_scr.shape, F32)
    p_b[...] = jnp.zeros(p_b.shape, BF16)
    al_b[...] = jnp.ones(al_b.shape, F32)

    def block_offset(kb):
        return pl.multiple_of(jnp.clip(kb, 0, qi) * tq, tq)

    def scores(kb, s_ref):
        k = k_ref[pl.ds(block_offset(kb), tq), :]
        for mi in range(2):
            s_ref[mi] = lax.dot_general(k, q_halves[mi], NT_DIMS, preferred_element_type=F32)

    def softmax(kb, s_ref, p_ref, al_ref):
        bias_t = bias_ref[0, jnp.clip(kb - qi + 2, 0, 3)]
        for mi in range(2):
            s = s_ref[mi] + bias_t
            m_old = m_scr[mi]
            m_new = jnp.maximum(m_old, jnp.max(s, axis=0, keepdims=True))
            alpha = jnp.exp(m_old - m_new)
            pr = jnp.exp(s - m_new)
            l_scr[mi] = alpha * l_scr[mi] + jnp.sum(pr, axis=0, keepdims=True)
            m_scr[mi] = m_new
            p_ref[mi] = pr.astype(BF16)
            al_ref[mi] = alpha

    def values(kb, p_ref, al_ref):
        vt = vt_ref[:, pl.ds(block_offset(kb), tq)]
        for mi in range(2):
            acc_scr[mi] = al_ref[mi] * acc_scr[mi] + jnp.dot(vt, p_ref[mi], preferred_element_type=F32)

    def pair(j, carry):
        a = 2 * j
        scores(a + 1, s_b)
        softmax(a, s_a, p_a, al_a)
        values(a - 1, p_b, al_b)
        scores(a + 2, s_a)
        softmax(a + 1, s_b, p_b, al_b)
        values(a, p_a, al_a)
        return carry

    n_pairs = (qi + 2) // 2
    scores(0, s_a)
    lax.fori_loop(0, n_pairs, pair, 0)
    values(2 * n_pairs - 1, p_b, al_b)
    lam = _lambda(lam_ref, lam_init)
    o_t = acc_scr[0] / l_scr[0] - lam * (acc_scr[1] / l_scr[1])
    ms = jnp.mean(o_t * o_t, axis=0, keepdims=True)
    o_t = o_t * lax.rsqrt(ms + EPS) * sgt_ref[...] * (1.0 - lam_init)
    o_ref[...] = o_t.T.astype(BF16)


def diff_attention_prompt(q, k, vt, n, l, bias_t, lam_vecs, subln_g, lam_init, tq):
    nq = l // tq
    kern = functools.partial(_attn_kernel, tq=tq, lam_init=lam_init)
    sg_t = jnp.broadcast_to(subln_g.reshape(ATT_DV, 1), (ATT_DV, tq))
    return pl.pallas_call(
        kern,
        grid=(n, ATT_HEADS, nq),
        in_specs=[pl.BlockSpec((4, ATT_DH), lambda b, h, i: (0, 0)),
                  pl.BlockSpec((tq, LANES), lambda b, h, i: (b * nq + i, h)),
                  pl.BlockSpec((l, LANES), lambda b, h, i: (b, h)),
                  pl.BlockSpec((LANES, l), lambda b, h, i: (h, b)),
                  pl.BlockSpec((1, 4, tq, tq), lambda b, h, i: (h, 0, 0, 0)),
                  pl.BlockSpec((ATT_DV, tq), lambda b, h, i: (0, 0))],
        out_specs=pl.BlockSpec((tq, LANES), lambda b, h, i: (b * nq + i, h)),
        out_shape=jax.ShapeDtypeStruct((n * l, ATT_WIDTH), BF16),
        scratch_shapes=[pltpu.VMEM((2, 1, tq), F32), pltpu.VMEM((2, 1, tq), F32),
                        pltpu.VMEM((2, ATT_DV, tq), F32),
                        pltpu.VMEM((2, tq, tq), F32), pltpu.VMEM((2, tq, tq), F32),
                        pltpu.VMEM((2, tq, tq), BF16), pltpu.VMEM((2, tq, tq), BF16),
                        pltpu.VMEM((2, 1, tq), F32), pltpu.VMEM((2, 1, tq), F32)],
        compiler_params=_params(("arbitrary", "arbitrary", "arbitrary")),
        name="diff_attn_prompt",
    )(lam_vecs, q, k, vt, bias_t, sg_t)


def _attn_decode_kernel(pt_ref, lam_ref, q_ref, kn_ref, vn_ref, pmask_ref, nmask_ref, sg_ref, *rest,
                        pps, l, lam_init):
    k_refs = rest[:pps]
    v_refs = rest[pps:2 * pps]
    o_ref = rest[2 * pps]
    q2_scr, m_scr, l_scr, acc_scr = rest[2 * pps + 1:]
    s_idx = pl.program_id(1)
    ns = pl.num_programs(1)
    last = s_idx == ns - 1

    @pl.when(s_idx == 0)
    def _():
        qf = q_ref[0].astype(F32)
        lane = lax.broadcasted_iota(jnp.int32, (l, LANES), 1)
        pieces = []
        for h in range(ATT_HEADS):
            qh = qf[:, h * LANES:(h + 1) * LANES]
            pieces += [jnp.where(lane < ATT_DH, qh, 0.0), jnp.where(lane < ATT_DH, 0.0, qh)]
        q2_scr[...] = jnp.concatenate(pieces, axis=0).astype(BF16)
        m_scr[...] = jnp.full(m_scr.shape, NEG_INF, F32)
        l_scr[...] = jnp.zeros(l_scr.shape, F32)
        acc_scr[...] = jnp.zeros(acc_scr.shape, F32)

    def update(s, vb):
        m_old = m_scr[...]
        m_new = jnp.maximum(m_old, jnp.max(s, axis=-1, keepdims=True))
        alpha = jnp.exp(m_old - m_new)
        pr = jnp.exp(s - m_new)
        l_scr[...] = alpha * l_scr[...] + jnp.sum(pr, axis=-1, keepdims=True)
        acc_scr[...] = alpha * acc_scr[...] + jnp.dot(pr.astype(BF16), vb, preferred_element_type=F32)
        m_scr[...] = m_new

    q2 = q2_scr[...]
    s_parts = []
    for pi in range(pps):
        kb = k_refs[pi][0].astype(BF16)
        s = lax.dot_general(q2, kb, NT_DIMS, preferred_element_type=F32)
        mask = jnp.where(last, pmask_ref[1], pmask_ref[0]) if pi == pps - 1 else pmask_ref[0]
        s_parts.append(s + mask)
    vb = jnp.concatenate([vr[0].astype(BF16) for vr in v_refs], axis=0)
    update(jnp.concatenate(s_parts, axis=1), vb)

    @pl.when(last)
    def _():
        nk = kn_ref.shape[1]
        pad = jnp.zeros((PAGE - nk, LANES), F32)
        kn = jnp.concatenate([kn_ref[0], pad], axis=0).astype(BF16)
        vn = jnp.concatenate([vn_ref[0], pad], axis=0).astype(BF16)
        sn = lax.dot_general(q2, kn, NT_DIMS, preferred_element_type=F32) + nmask_ref[...]
        update(sn, vn)
        lam = _lambda(lam_ref, lam_init)
        outs = []
        for h in range(ATT_HEADS):
            r0 = 2 * h * l
            o0 = acc_scr[r0:r0 + l, :] / l_scr[r0:r0 + l, :]
            o1 = acc_scr[r0 + l:r0 + 2 * l, :] / l_scr[r0 + l:r0 + 2 * l, :]
            outs.append(_subln(o0 - lam * o1, sg_ref, lam_init))
        o_ref[0] = jnp.concatenate(outs, axis=1).astype(BF16)


def diff_attention_decode(q, k_new, v_new, cache_k, cache_v, page_table, pmask, nmask, lam_vecs, subln_g, lam_init):
    n, l, _ = q.shape
    n_pages = page_table.shape[1]
    pps = next(c for c in (DECODE_PAGES_PER_STEP, 4, 2, 1) if n_pages % c == 0)
    ns = n_pages // pps
    rows = 2 * ATT_HEADS * l
    pk = PAGE * ATT_HEADS
    kern = functools.partial(_attn_decode_kernel, pps=pps, l=l, lam_init=lam_init)

    def page_spec(pi):
        return pl.BlockSpec((1, pk, LANES), lambda b, s, pt: (pt[b * n_pages + s * pps + pi], 0, 0))

    seq = lambda b, s, pt: (b, 0, 0)
    grid_spec = pltpu.PrefetchScalarGridSpec(
        num_scalar_prefetch=1,
        grid=(n, ns),
        in_specs=[pl.BlockSpec((4, ATT_DH), lambda b, s, pt: (0, 0)),
                  pl.BlockSpec((1, l, ATT_WIDTH), seq),
                  pl.BlockSpec((1, l * ATT_HEADS, LANES), seq),
                  pl.BlockSpec((1, l * ATT_HEADS, LANES), seq),
                  pl.BlockSpec((2, rows, pk), lambda b, s, pt: (0, 0, 0)),
                  pl.BlockSpec((rows, PAGE), lambda b, s, pt: (0, 0)),
                  pl.BlockSpec((1, ATT_DV), lambda b, s, pt: (0, 0))]
                 + [page_spec(pi) for pi in range(pps)] * 2,
        out_specs=pl.BlockSpec((1, l, ATT_WIDTH), seq),
        scratch_shapes=[pltpu.VMEM((rows, LANES), BF16), pltpu.VMEM((rows, 1), F32),
                        pltpu.VMEM((rows, 1), F32), pltpu.VMEM((rows, ATT_DV), F32)])
    return pl.pallas_call(
        kern,
        grid_spec=grid_spec,
        out_shape=jax.ShapeDtypeStruct((n, l, ATT_WIDTH), BF16),
        compiler_params=_params(("arbitrary", "arbitrary")),
        name="diff_attn_decode",
    )(page_table.reshape(-1), lam_vecs, q, k_new, v_new, pmask, nmask, subln_g.reshape(1, ATT_DV),
      *([cache_k] * pps), *([cache_v] * pps))


def decode_masks(bt, l):
    rows = 2 * ATT_HEADS * l
    r = np.arange(rows)
    row_head, row_tok = r // (2 * l), r % l
    c = np.arange(PAGE * ATT_HEADS)
    head_ok = (c[None, :] % ATT_HEADS) == row_head[:, None]
    neg = np.where(head_ok, 0.0, -np.inf).astype(np.float32)
    cn = np.arange(PAGE)
    key_n = cn // ATT_HEADS
    new_ok = ((cn[None, :] % ATT_HEADS) == row_head[:, None]) & (key_n[None, :] <= row_tok[:, None]) \
        & (key_n[None, :] < l)
    neg_n = np.where(new_ok, 0.0, -np.inf).astype(np.float32)
    per_row = jnp.broadcast_to(bt[:, :, None], (ATT_HEADS, 2, 2, l, PAGE))
    last = jnp.repeat(per_row[:, 0].reshape(rows, PAGE), ATT_HEADS, axis=1)
    new = jnp.repeat(per_row[:, 1].reshape(rows, PAGE)[:, :PAGE // ATT_HEADS], ATT_HEADS, axis=1)
    pmask = jnp.stack([jnp.asarray(neg), last + neg])
    nmask = new + neg_n
    return pmask, nmask


_PEER_CAND = [(i, j) for i in range(PEER_NTOP) for j in range(PEER_NTOP) if (i + 1) * (j + 1) <= PEER_NTOP]


def _route_kernel(hn_ref, wq_ref, sk_ref, r2_o, e2_o, n1_o, e1_o, q_scr, s_scr, a_scr, b_scr):
    q_scr[...] = jnp.dot(hn_ref[...], wq_ref[...], preferred_element_type=F32)
    for h in range(PEER_HEADS):
        for m in range(2):
            c0 = (2 * h + m) * LANES
            qs = q_scr[:, c0:c0 + LANES].astype(BF16)
            s = lax.dot_general(sk_ref[m], qs, NT_DIMS, preferred_element_type=F32)
            s_scr[m, h] = s
            top_scr = a_scr if m == 0 else b_scr
            x = s
            rank = jnp.full(s.shape, float(PEER_NTOP), F32)
            for r in range(PEER_NTOP):
                mx = jnp.max(x, axis=0, keepdims=True)
                top_scr[r, h:h + 1, :] = mx
                hit = x == mx
                if m == 1:
                    rank = jnp.where(hit, float(r), rank)
                if r + 1 < PEER_NTOP:
                    x = jnp.where(hit, NEG_INF, x)
            if m == 1:
                r2_o[h] = rank.astype(BF16)
    a_top = [a_scr[r] for r in range(PEER_NTOP)]
    b_top = [b_scr[r] for r in range(PEER_NTOP)]
    cands = [a_top[i] + b_top[j] for (i, j) in _PEER_CAND]
    xs = list(cands)
    ranked = []
    for r in range(PEER_NTOP):
        mx = functools.reduce(jnp.maximum, xs)
        ranked.append(mx)
        if r + 1 < PEER_NTOP:
            xs = [jnp.where(x == mx, NEG_INF, x) for x in xs]
    thr = 0.5 * (ranked[PEER_TOPK - 1] + ranked[PEER_TOPK])
    top = a_top[0] + b_top[0]
    z = functools.reduce(lambda u, w: u + w, [jnp.where(c >= thr, jnp.exp(c - top), 0.0) for c in cands])
    scale = 0.5 / z
    for h in range(PEER_HEADS):
        s1 = s_scr[0, h]
        s2 = s_scr[1, h]
        a0 = a_top[0][h:h + 1]
        b0 = b_top[0][h:h + 1]
        e2_o[h] = jnp.exp(s2 - b0).astype(BF16)
        need = thr[h:h + 1] - s1
        n1 = jnp.zeros(s1.shape, F32)
        for b in range(PEER_NTOP):
            n1 = jnp.where(b_top[b][h:h + 1] >= need, float(b + 1), n1)
        n1_o[h] = n1
        e1_o[h] = jnp.exp(s1 - a0) * scale[h:h + 1]


def peer_route(hn, wq, sk):
    t, d = hn.shape
    tm = min(256, t)
    spec = pl.BlockSpec((PEER_HEADS, PEER_NKEYS, tm), lambda i: (0, 0, i))
    shape = lambda dt: jax.ShapeDtypeStruct((PEER_HEADS, PEER_NKEYS, t), dt)
    return pl.pallas_call(
        _route_kernel,
        grid=(t // tm,),
        in_specs=[pl.BlockSpec((tm, d), lambda i: (i, 0)),
                  pl.BlockSpec(wq.shape, lambda i: (0, 0)),
                  pl.BlockSpec(sk.shape, lambda i: (0, 0, 0))],
        out_specs=[spec, spec, spec, spec],
        out_shape=[shape(BF16), shape(BF16), shape(F32), shape(F32)],
        scratch_shapes=[pltpu.VMEM((tm, wq.shape[1]), F32),
                        pltpu.VMEM((2, PEER_HEADS, PEER_NKEYS, tm), F32),
                        pltpu.VMEM((PEER_NTOP, PEER_HEADS, tm), F32),
                        pltpu.VMEM((PEER_NTOP, PEER_HEADS, tm), F32)],
        compiler_params=_params(("arbitrary",)),
        name="peer_route",
    )(hn, wq, sk)


GELU_C0 = math.sqrt(2.0 / math.pi)
GELU_C1 = GELU_C0 * 0.044715


def _peer_kernel(hn_ref, u_ref, vt_ref, r2_ref, e2_ref, n1_ref, e1_ref, x1_ref, g2_ref, o_ref,
                 acc_scr, act_scr, wa_scr):
    e = pl.program_id(2)
    tm = hn_ref.shape[0]

    @pl.when(e == 0)
    def _():
        acc_scr[...] = jnp.zeros(acc_scr.shape, F32)

    act_scr[...] = lax.dot_general(u_ref[...], hn_ref[...], NT_DIMS, preferred_element_type=F32)
    for il in range(PEER_IB):
        rows = slice(il * PEER_NKEYS, (il + 1) * PEER_NKEYS)
        w = None
        for h in range(PEER_HEADS):
            n1 = jnp.broadcast_to(n1_ref[h, il:il + 1, :], (PEER_NKEYS, tm)).astype(BF16)
            e1 = jnp.broadcast_to(e1_ref[h, il:il + 1, :], (PEER_NKEYS, tm)).astype(BF16)
            e2 = e2_ref[h]
            term = jnp.where(r2_ref[h] < n1, e2 * e1, jnp.zeros_like(e2))
            w = term if w is None else w + term
        x = act_scr[rows, :]
        inner = x * (GELU_C0 + GELU_C1 * (x * x))
        wa_scr[rows, :] = w * (x * (1.0 + jnp.tanh(inner))).astype(BF16)
    acc_scr[...] += jnp.dot(vt_ref[...], wa_scr[...], preferred_element_type=F32)

    @pl.when(e == pl.num_programs(2) - 1)
    def _():
        y = acc_scr[...].T
        o_ref[...] = x1_ref[...] + g2_ref[...] * y.reshape(x1_ref.shape)


def peer_mix(hn, u_b, vt_b, r2t, e2t, n1t, e1t, x1, g2):
    n, l, d = x1.shape
    nb, lb = _row_blocks(n, l)
    tm = nb * lb
    lk = l // lb
    ne = PEER_EXPERTS // PEER_EB
    tok = lambda i, k, e: (i * lk + k, 0)
    keys2 = pl.BlockSpec((PEER_HEADS, PEER_NKEYS, tm), lambda i, k, e: (0, 0, i * lk + k))
    keys1 = pl.BlockSpec((PEER_HEADS, PEER_IB, tm), lambda i, k, e: (0, e, i * lk + k))
    return pl.pallas_call(
        _peer_kernel,
        grid=(n // nb, lk, ne),
        in_specs=[pl.BlockSpec((tm, d), tok),
                  pl.BlockSpec((PEER_EB, d), lambda i, k, e: (e, 0)),
                  pl.BlockSpec((d, PEER_EB), lambda i, k, e: (0, e)),
                  keys2, keys2, keys1, keys1,
                  pl.BlockSpec((nb, lb, d), lambda i, k, e: (i, k, 0)),
                  pl.BlockSpec((nb, 1, d), lambda i, k, e: (i, 0, 0))],
        out_specs=pl.BlockSpec((nb, lb, d), lambda i, k, e: (i, k, 0)),
        out_shape=jax.ShapeDtypeStruct((n, l, d), F32),
        scratch_shapes=[pltpu.VMEM((d, tm), F32), pltpu.VMEM((PEER_EB, tm), F32),
                        pltpu.VMEM((PEER_EB, tm), BF16)],
        compiler_params=_params(("arbitrary", "arbitrary", "arbitrary")),
        name="peer_mix",
    )(hn, u_b, vt_b, r2t, e2t, n1t, e1t, x1, g2)


def peer_ffn_residual(hn, x1, g2, wq_b, sk_b, u_b, vt_b):
    r2t, e2t, n1t, e1t = peer_route(hn, wq_b, sk_b)
    return peer_mix(hn, u_b, vt_b, r2t, e2t, n1t, e1t, x1, g2)


def _trunk(x, mods, pos0, conv_prev, ssm_prev, pool_prev, paged_kv, p, w):
    n, l, d = x.shape
    t = n * l
    sh1, sc1, g1, sh2, sc2, g2 = mods[0]
    proj = norm_inproj(x, p["norm_mix_g"][0], sc1, sh1, w["hyb_in"], tn=HYB_PROJ // 3)
    mixed, conv_new, ssm_new, pool_new = ssd_pool(proj, n, l, conv_prev, ssm_prev, pool_prev, pos0, w["ssm"])
    x1, hn = outproj_residual(mixed, w["hyb_out"], x, g1, p["norm_ffn_g"][0], sc2, sh2)
    x2 = peer_ffn_residual(hn, x1, g2, w["peer_wq"][0], w["peer_sk"][0], w["peer_u"][0], w["peer_vt"][0])

    sh1, sc1, g1, sh2, sc2, g2 = mods[1]
    qb, kf, kb, vf, vt = norm_qkv(x2, p["norm_mix_g"][1], sc1, sh1, w["att_in"], p["att_q_g"][0], p["att_k_g"][0])
    lam_init = 0.8 - 0.6 * math.exp(-0.3 * 1)
    lam_vecs = jnp.stack([p["att_lambda_q1"][0], p["att_lambda_k1"][0],
                          p["att_lambda_q2"][0], p["att_lambda_k2"][0]])
    if paged_kv is None:
        tq = min(ATT_TQ, l)
        bias_t = bias_tiles(p["rel_bias"], (("zero", 0), ("bias", tq), ("bias_causal", 0), ("neg", 0)),
                            tq, tq, sign=-1)
        o = diff_attention_prompt(qb, kb, vt, n, l, bias_t, lam_vecs, p["att_subln_g"][0], lam_init, tq)
    else:
        cache_k, cache_v, page_table = paged_kv
        past_len = page_table.shape[1] * PAGE
        bt = bias_tiles(p["rel_bias"], (("bias", pos0 - (past_len - PAGE)), ("bias", pos0 - past_len)),
                        l, PAGE)
        pmask, nmask = decode_masks(bt, l)
        o = diff_attention_decode(qb.reshape(n, l, ATT_WIDTH), kf.reshape(n, l * ATT_HEADS, ATT_DV),
                                  vf.reshape(n, l * ATT_HEADS, ATT_DV), cache_k, cache_v, page_table,
                                  pmask, nmask, lam_vecs, p["att_subln_g"][0], lam_init).reshape(t, ATT_WIDTH)
    x3, hn = outproj_residual(o, w["att_out"], x2, g1, p["norm_ffn_g"][1], sc2, sh2)
    x4 = peer_ffn_residual(hn, x3, g2, w["peer_wq"][1], w["peer_sk"][1], w["peer_u"][1], w["peer_vt"][1])
    k_new = kf.reshape(1, n, l, ATT_HEADS, 2 * ATT_DH)
    v_new = vf.reshape(1, n, l, ATT_HEADS, ATT_DV)
    return x4, conv_new[None], ssm_new[None], pool_new[None], k_new, v_new


def kernel(x_prompt, x_sample, state_conv, state_ssm, state_pool, cache_k, cache_v, page_table,
           c_prompt, c_sample, rel_bias, norm_mix_g, norm_ffn_g, ada_w, ada_b,
           hyb_w_in, ssm_conv_w, ssm_conv_b, ssm_dt_bias, ssm_a_log, ssm_d, ssm_norm_g,
           pool_w, pool_scale, hyb_w_out, att_w_in, att_q_g, att_k_g,
           att_lambda_q1, att_lambda_k1, att_lambda_q2, att_lambda_k2, att_subln_g, att_w_out,
           peer_wq, peer_subkeys, peer_u, peer_v):
    p = {
        "rel_bias": rel_bias, "norm_mix_g": norm_mix_g, "norm_ffn_g": norm_ffn_g,
        "att_q_g": att_q_g, "att_k_g": att_k_g,
        "att_lambda_q1": att_lambda_q1, "att_lambda_k1": att_lambda_k1,
        "att_lambda_q2": att_lambda_q2, "att_lambda_k2": att_lambda_k2, "att_subln_g": att_subln_g,
    }
    w_in = hyb_w_in[0]
    z_w = w_in[:, :SSM_D_INNER]
    xbc_w = w_in[:, SSM_D_INNER:SSM_D_INNER + SSM_CONV_DIM]
    dt_w = w_in[:, SSM_D_INNER + SSM_CONV_DIM:SSM_D_INNER + SSM_CONV_DIM + SSM_HEADS]
    u_w = w_in[:, SSM_D_INNER + SSM_CONV_DIM + SSM_HEADS:]
    hyb_in = jnp.concatenate([z_w, xbc_w, u_w, jnp.pad(dt_w, ((0, 0), (0, LANES - SSM_HEADS)))], axis=1)
    w = {
        "hyb_in": hyb_in.astype(BF16),
        "hyb_out": hyb_w_out[0].astype(BF16),
        "att_in": att_w_in[0].astype(BF16),
        "att_out": att_w_out[0].astype(BF16),
        "peer_wq": peer_wq.astype(BF16),
        "peer_sk": peer_subkeys.astype(BF16),
        "peer_u": peer_u.astype(BF16),
        "peer_vt": jnp.swapaxes(peer_v, 1, 2).astype(BF16),
        "ssm": {"ssm_conv_w": ssm_conv_w[0], "ssm_conv_b": ssm_conv_b[0], "ssm_dt_bias": ssm_dt_bias[0],
                "ssm_a_log": ssm_a_log[0], "ssm_d": ssm_d[0], "ssm_norm_g": ssm_norm_g[0],
                "pool_w": pool_w[0], "pool_scale": pool_scale[0]},
    }
    nb, ns = x_prompt.shape[0], x_sample.shape[0]
    d = x_prompt.shape[2]
    rows = nb + ns
    rpad = -rows % SUBLANES
    c_all = jnp.concatenate([c_prompt, c_sample, jnp.zeros((rpad, d), F32)], axis=0)
    mod = ada_mod(c_all, ada_w, ada_b)

    def mods_for(lo, cnt):
        return [tuple(mod[i, lo:lo + cnt, k * d:(k + 1) * d].reshape(cnt, 1, d) for k in range(6))
                for i in range(mod.shape[0])]

    conv0 = jnp.zeros((nb, SSM_CONV - 1, SSM_CONV_DIM), F32)
    ssm0 = jnp.zeros((nb, SSM_HEADS, SSM_HEAD_DIM, SSM_D_STATE), F32)
    pool0 = jnp.zeros((nb, POOL_HIST, POOL_DIM), F32)
    yp, conv_p, ssm_p, pool_p, k_p, v_p = _trunk(x_prompt, mods_for(0, nb), 0, conv0, ssm0, pool0, None, p, w)
    past_len = page_table.shape[1] * cache_k.shape[2]
    n_phys = cache_k.shape[1]
    ys, conv_s, ssm_s, pool_s, k_s, v_s = _trunk(
        x_sample, mods_for(nb, ns), past_len, state_conv[0], state_ssm[0], state_pool[0],
        (cache_k[0].reshape(n_phys, PAGE * ATT_HEADS, ATT_DV), cache_v[0].reshape(n_phys, PAGE * ATT_HEADS, ATT_DV),
         page_table),
        p, w)
    return (yp, ys, conv_p, ssm_p, pool_p, k_p, v_p, conv_s, ssm_s, pool_s, k_s, v_s)
```

```python
import functools
import math

import numpy as np
import jax
import jax.numpy as jnp
from jax import lax
from jax.experimental import pallas as pl
from jax.experimental.pallas import tpu as pltpu

F32 = jnp.float32
BF16 = jnp.bfloat16

D_MODEL = 1024
EPS = 1e-6
LANES = 128
SUBLANES = 8

SSM_D_INNER = 2048
SSM_HEAD_DIM = 64
SSM_HEADS = 32
SSM_GROUPS = 2
SSM_D_STATE = 128
SSM_CONV = 4
SSM_CHUNK = 128
SSM_CONV_DIM = 2560
GROUP_W = SSM_D_INNER // SSM_GROUPS

POOL_WINDOWS = (2, 4, 8, 16)
POOL_DIM = 1024
POOL_GROUP = 256
POOL_HIST = 15
POOL_HALO = 16
CONV_HALO = 8

HYB_PROJ = SSM_D_INNER + SSM_CONV_DIM + POOL_DIM + LANES
HYB_MIX = SSM_D_INNER + POOL_DIM

ATT_HEADS = 8
ATT_DH = 64
ATT_DV = 128
ATT_WIDTH = 1024
REL_BUCKETS = 32
REL_MAX_DIST = 128
PAGE = 128
ATT_TQ = 512
DECODE_PAGES_PER_STEP = 16

PEER_HEADS = 8
PEER_NKEYS = 128
PEER_EXPERTS = PEER_NKEYS * PEER_NKEYS
PEER_TOPK = 16
PEER_NTOP = PEER_TOPK + 1
PEER_EB = 1024
PEER_IB = PEER_EB // PEER_NKEYS

NEG_INF = float("-inf")
LOG2E = math.log2(math.e)
NT_DIMS = (((1,), (1,)), ((), ()))

VMEM_LIMIT = 56 * 1024 * 1024


def _params(sem):
    return pltpu.CompilerParams(dimension_semantics=sem, vmem_limit_bytes=VMEM_LIMIT)


def _sigmoid(x):
    return 1.0 / (1.0 + jnp.exp(-x))


def _silu(x):
    return x * _sigmoid(x)


def _split3(x):
    hi = x.astype(BF16)
    r1 = x - hi.astype(F32)
    mid = r1.astype(BF16)
    lo = (r1 - mid.astype(F32)).astype(BF16)
    return hi, mid, lo


def _norm_mod(x3, g_ref, sc_ref, sh_ref):
    ms = jnp.mean(x3 * x3, axis=-1, keepdims=True)
    y = x3 * lax.rsqrt(ms + EPS) * g_ref[...]
    return y * (1.0 + sc_ref[...]) + sh_ref[...]


def _row_blocks(n, l):
    if l >= 512:
        return 1, 512
    if l >= SUBLANES and l % SUBLANES == 0 and l < 512:
        if l == SUBLANES:
            nb = min(n, 64)
            while n % nb:
                nb //= 2
            return nb, l
        return 1, l
    raise ValueError(f"unsupported sequence length {l}")


def _ada_kernel(c_ref, w_ref, b_ref, o_ref):
    ca = _silu(c_ref[...]).astype(BF16)
    o_ref[0] = jnp.dot(ca, w_ref[0].astype(BF16), preferred_element_type=F32) + b_ref[0]


def ada_mod(c_all, ada_w, ada_b):
    r = c_all.shape[0]
    depth, d, n6 = ada_w.shape
    tn = 512
    return pl.pallas_call(
        _ada_kernel,
        grid=(depth, n6 // tn),
        in_specs=[pl.BlockSpec((r, d), lambda i, j: (0, 0)),
                  pl.BlockSpec((1, d, tn), lambda i, j: (i, 0, j)),
                  pl.BlockSpec((1, 1, tn), lambda i, j: (i, 0, j))],
        out_specs=pl.BlockSpec((1, r, tn), lambda i, j: (i, 0, j)),
        out_shape=jax.ShapeDtypeStruct((depth, r, n6), F32),
        compiler_params=_params(("arbitrary", "arbitrary")),
        name="ada_mod",
    )(c_all, ada_w, ada_b.reshape(depth, 1, n6))


def _inproj_kernel(x_ref, g_ref, sc_ref, sh_ref, w_ref, o_ref, h_scr):
    @pl.when(pl.program_id(2) == 0)
    def _():
        h = _norm_mod(x_ref[...], g_ref, sc_ref, sh_ref)
        h_scr[...] = h.reshape(h_scr.shape).astype(BF16)

    o_ref[...] = jnp.dot(h_scr[...], w_ref[...], preferred_element_type=F32)


def norm_inproj(x, g, sc, sh, w, tn):
    n, l, d = x.shape
    nb, lb = _row_blocks(n, l)
    m = nb * lb
    lk = l // lb
    nout = w.shape[1]
    return pl.pallas_call(
        _inproj_kernel,
        grid=(n // nb, lk, nout // tn),
        in_specs=[pl.BlockSpec((nb, lb, d), lambda i, k, j: (i, k, 0)),
                  pl.BlockSpec((1, d), lambda i, k, j: (0, 0)),
                  pl.BlockSpec((nb, 1, d), lambda i, k, j: (i, 0, 0)),
                  pl.BlockSpec((nb, 1, d), lambda i, k, j: (i, 0, 0)),
                  pl.BlockSpec((d, tn), lambda i, k, j: (0, j))],
        out_specs=pl.BlockSpec((m, tn), lambda i, k, j: (i * lk + k, j)),
        out_shape=jax.ShapeDtypeStruct((n * l, nout), F32),
        scratch_shapes=[pltpu.VMEM((m, d), BF16)],
        compiler_params=_params(("arbitrary", "arbitrary", "arbitrary")),
        name="norm_inproj",
    )(x, g.reshape(1, d), sc, sh, w)


def _seg_rms(y, g128, bd):
    sq = y * y
    hi = sq.astype(BF16)
    lo = (sq - hi.astype(F32)).astype(BF16)
    outs = []
    for cb in range(y.shape[1] // LANES):
        sl = slice(cb * LANES, (cb + 1) * LANES)
        ms = (jnp.dot(hi[:, sl], bd, preferred_element_type=F32)
              + jnp.dot(lo[:, sl], bd, preferred_element_type=F32))
        outs.append(y[:, sl] * lax.rsqrt(ms + EPS) * g128)
    return jnp.concatenate(outs, axis=1)


def _qkv_kernel(x_ref, g_ref, sc_ref, sh_ref, w_ref, qg_ref, kg_ref, bd_ref,
                q_o, kf_o, kb_o, vf_o, vt_o, h_scr):
    j = pl.program_id(2)

    @pl.when(j == 0)
    def _():
        h = _norm_mod(x_ref[...], g_ref, sc_ref, sh_ref)
        h_scr[...] = h.reshape(h_scr.shape).astype(BF16)

    y = jnp.dot(h_scr[...], w_ref[...], preferred_element_type=F32)

    @pl.when(j == 0)
    def _():
        q_o[...] = (_seg_rms(y, qg_ref[...], bd_ref[...]) * (ATT_DH ** -0.5 * LOG2E)).astype(BF16)

    @pl.when(j == 1)
    def _():
        kn = _seg_rms(y, kg_ref[...], bd_ref[...])
        kf_o[...] = kn
        kb_o[...] = kn.astype(BF16)

    @pl.when(j == 2)
    def _():
        vf_o[...] = y
        vt_o[...] = y.T.astype(BF16)


def norm_qkv(x, g, sc, sh, w, q_g, k_g):
    n, l, d = x.shape
    nb, lb = _row_blocks(n, l)
    m = nb * lb
    lk = l // lb
    t = n * l
    seg = np.arange(LANES) // ATT_DH
    bd = jnp.asarray((seg[:, None] == seg[None, :]).astype(np.float32) / ATT_DH, BF16)
    row = lambda i, k, j: (i * lk + k, 0)
    const2 = lambda i, k, j: (0, 0)
    outs = pl.pallas_call(
        _qkv_kernel,
        grid=(n // nb, lk, 3),
        in_specs=[pl.BlockSpec((nb, lb, d), lambda i, k, j: (i, k, 0)),
                  pl.BlockSpec((1, d), const2),
                  pl.BlockSpec((nb, 1, d), lambda i, k, j: (i, 0, 0)),
                  pl.BlockSpec((nb, 1, d), lambda i, k, j: (i, 0, 0)),
                  pl.BlockSpec((d, ATT_WIDTH), lambda i, k, j: (0, j)),
                  pl.BlockSpec((1, LANES), const2),
                  pl.BlockSpec((1, LANES), const2),
                  pl.BlockSpec((LANES, LANES), const2)],
        out_specs=[pl.BlockSpec((m, ATT_WIDTH), row)] * 4
                  + [pl.BlockSpec((ATT_WIDTH, m), lambda i, k, j: (0, i * lk + k))],
        out_shape=[jax.ShapeDtypeStruct((t, ATT_WIDTH), BF16),
                   jax.ShapeDtypeStruct((t, ATT_WIDTH), F32),
                   jax.ShapeDtypeStruct((t, ATT_WIDTH), BF16),
                   jax.ShapeDtypeStruct((t, ATT_WIDTH), F32),
                   jax.ShapeDtypeStruct((ATT_WIDTH, t), BF16)],
        scratch_shapes=[pltpu.VMEM((m, d), BF16)],
        compiler_params=_params(("arbitrary", "arbitrary", "arbitrary")),
        name="norm_qkv",
    )(x, g.reshape(1, d), sc, sh, w,
      jnp.tile(q_g, 2).reshape(1, LANES), jnp.tile(k_g, 2).reshape(1, LANES), bd)
    return outs


def _outproj_kernel(a_ref, w_ref, x_ref, g1_ref, gn_ref, sc_ref, sh_ref, x1_o, hn_o):
    y = jnp.dot(a_ref[...], w_ref[...], preferred_element_type=F32)
    x1 = x_ref[...] + g1_ref[...] * y.reshape(x_ref.shape)
    x1_o[...] = x1
    hn_o[...] = _norm_mod(x1, gn_ref, sc_ref, sh_ref).reshape(hn_o.shape).astype(BF16)


def outproj_residual(a, w, x, g1, gn, sc, sh):
    n, l, d = x.shape
    nb, lb = _row_blocks(n, l)
    m = nb * lb
    lk = l // lb
    kdim = a.shape[1]
    seq = lambda i, k: (i, 0, 0)
    return pl.pallas_call(
        _outproj_kernel,
        grid=(n // nb, lk),
        in_specs=[pl.BlockSpec((m, kdim), lambda i, k: (i * lk + k, 0)),
                  pl.BlockSpec((kdim, d), lambda i, k: (0, 0)),
                  pl.BlockSpec((nb, lb, d), lambda i, k: (i, k, 0)),
                  pl.BlockSpec((nb, 1, d), seq),
                  pl.BlockSpec((1, d), lambda i, k: (0, 0)),
                  pl.BlockSpec((nb, 1, d), seq),
                  pl.BlockSpec((nb, 1, d), seq)],
        out_specs=[pl.BlockSpec((nb, lb, d), lambda i, k: (i, k, 0)),
                   pl.BlockSpec((m, d), lambda i, k: (i * lk + k, 0))],
        out_shape=[jax.ShapeDtypeStruct((n, l, d), F32),
                   jax.ShapeDtypeStruct((n * l, d), BF16)],
        compiler_params=_params(("arbitrary", "arbitrary")),
        name="outproj_residual",
    )(a, w, x, g1, gn.reshape(1, d), sc, sh)


def _ssd_pool_kernel(proj_ref, convp_ref, ssmp_ref, poolp_ref, convw_ref, convb_ref, dtb_ref, alog_ref,
                     dexp_ref, normg_ref, poolw_ref, pools_ref, tril_ref, rexp_ref,
                     mixed_o, convn_o, ssmn_o, pooln_o,
                     ext_c, ext_p, ht_scr, *, lr, pos0):
    q = SSM_CHUNK
    c = pl.program_id(1)
    nc = pl.num_programs(1)

    @pl.when(c == 0)
    def _():
        ext_c[0:CONV_HALO, :] = convp_ref[0]
        ext_p[0:POOL_HALO, :] = poolp_ref[0]
        for g in range(SSM_GROUPS):
            ht_scr[g] = ssmp_ref[0, g * GROUP_W:(g + 1) * GROUP_W, :].T

    xbc_cols = slice(SSM_D_INNER, SSM_D_INNER + SSM_CONV_DIM)
    u_cols = slice(SSM_D_INNER + SSM_CONV_DIM, SSM_D_INNER + SSM_CONV_DIM + POOL_DIM)
    dt_cols = slice(SSM_D_INNER + SSM_CONV_DIM + POOL_DIM, HYB_PROJ)

    ext_c[CONV_HALO:CONV_HALO + lr, :] = proj_ref[:, xbc_cols]
    ext_p[POOL_HALO:POOL_HALO + lr, :] = proj_ref[:, u_cols]
    if lr < q:
        ext_c[CONV_HALO + lr:CONV_HALO + q, :] = jnp.zeros((q - lr, SSM_CONV_DIM), F32)
        ext_p[POOL_HALO + lr:POOL_HALO + q, :] = jnp.zeros((q - lr, POOL_DIM), F32)

    acc = convb_ref[...] + ext_c[pl.ds(CONV_HALO - 3, q), :] * convw_ref[0:1, :]
    for tap in range(1, SSM_CONV):
        acc = acc + ext_c[pl.ds(CONV_HALO - 3 + tap, q), :] * convw_ref[tap:tap + 1, :]
    xbc = _silu(acc)
    xs = xbc[:, :SSM_D_INNER]

    row = lax.broadcasted_iota(jnp.int32, (q, LANES), 0)
    dt_raw = proj_ref[:, dt_cols] + dtb_ref[...]
    if lr < q:
        dt_raw = jnp.concatenate([dt_raw, jnp.zeros((q - lr, LANES), F32)], axis=0)
    dt = jnp.maximum(dt_raw, 0.0) + jnp.log(1.0 + jnp.exp(-jnp.abs(dt_raw)))
    dt = jnp.where(row < lr, dt, 0.0)
    a_neg = -jnp.exp(alog_ref[...])
    d_a = dt * a_neg

    tril = tril_ref[...]
    rexp = rexp_ref[...]
    a_cs = sum(jnp.dot(tril, p, preferred_element_type=F32) for p in _split3(d_a))
    acs_e = sum(jnp.dot(p, rexp, preferred_element_type=F32) for p in _split3(a_cs))
    dt_e = sum(jnp.dot(p, rexp, preferred_element_type=F32) for p in _split3(dt))
    a_cs_t = a_cs.T

    xdt = xs * dt_e
    alast_e = acs_e[lr - 1:lr, :]
    xdtw = (xdt * jnp.exp(alast_e - acs_e)).astype(BF16)
    ea_e = jnp.exp(acs_e)
    xdt_b = xdt.astype(BF16)

    ii = lax.broadcasted_iota(jnp.int32, (q, q), 0)
    jj = lax.broadcasted_iota(jnp.int32, (q, q), 1)
    causal = ii >= jj
    lane = lax.broadcasted_iota(jnp.int32, (q, LANES), 1)
    even_half = lane < SSM_HEAD_DIM
    zero_b = jnp.zeros((q, LANES), BF16)

    ys = []
    for g in range(SSM_GROUPS):
        bg = xbc[:, SSM_D_INNER + g * SSM_D_STATE:SSM_D_INNER + (g + 1) * SSM_D_STATE]
        cg = xbc[:, SSM_D_INNER + SSM_GROUPS * SSM_D_STATE + g * SSM_D_STATE:
                 SSM_D_INNER + SSM_GROUPS * SSM_D_STATE + (g + 1) * SSM_D_STATE].astype(BF16)
        bg_t = bg.T.astype(BF16)
        gcols = slice(g * GROUP_W, (g + 1) * GROUP_W)
        cb = jnp.dot(cg, bg_t, preferred_element_type=F32)
        h_in = ht_scr[g]
        y_off = jnp.dot(cg, h_in.astype(BF16), preferred_element_type=F32) * ea_e[:, gcols]
        st = jnp.dot(bg_t, xdtw[:, gcols], preferred_element_type=F32)
        ht_scr[g] = h_in * jnp.exp(alast_e[:, gcols]) + st
        pieces = []
        for pr in range(GROUP_W // LANES):
            ms = []
            for hh in range(2):
                h = g * (SSM_HEADS // SSM_GROUPS) + 2 * pr + hh
                seg = a_cs[:, h:h + 1] - a_cs_t[h:h + 1, :]
                ms.append((cb * jnp.exp(jnp.where(causal, seg, NEG_INF))).astype(BF16))
            xblk = xdt_b[:, g * GROUP_W + pr * LANES:g * GROUP_W + (pr + 1) * LANES]
            rhs = jnp.concatenate([jnp.where(even_half, xblk, zero_b),
                                   jnp.where(even_half, zero_b, xblk)], axis=0)
            pieces.append(jnp.dot(jnp.concatenate(ms, axis=1), rhs, preferred_element_type=F32))
        ys.append(jnp.concatenate(pieces, axis=1) + y_off)
    y = jnp.concatenate(ys, axis=1) + dexp_ref[...] * xs
    y = y * _silu(proj_ref[:, :SSM_D_INNER]) if lr == q else (
        y * _silu(jnp.concatenate([proj_ref[:, :SSM_D_INNER], jnp.zeros((q - lr, SSM_D_INNER), F32)], axis=0)))
    for g in range(SSM_GROUPS):
        gcols = slice(g * GROUP_W, (g + 1) * GROUP_W)
        yg = y[:, gcols]
        ms = jnp.mean(yg * yg, axis=-1, keepdims=True)
        yn = yg * lax.rsqrt(ms + EPS) * normg_ref[:, gcols]
        mixed_o[:, gcols] = yn[:lr].astype(BF16)

    pos = pos0 + c * q + lax.broadcasted_iota(jnp.int32, (q, POOL_GROUP), 0)
    for gi, w in enumerate(POOL_WINDOWS):
        cols = slice(gi * POOL_GROUP, (gi + 1) * POOL_GROUP)
        cur = ext_p[pl.ds(POOL_HALO, q), cols]
        wsum = cur
        for k in range(1, w):
            wsum = wsum + ext_p[pl.ds(POOL_HALO - k, q), cols]
        cnt = jnp.minimum(w, pos + 1).astype(F32)
        pooled = wsum / cnt - cur
        yp = jnp.dot(pooled.astype(BF16), poolw_ref[gi], preferred_element_type=F32) * pools_ref[:, cols]
        mixed_o[:, SSM_D_INNER + gi * POOL_GROUP:SSM_D_INNER + (gi + 1) * POOL_GROUP] = yp[:lr].astype(BF16)

    new_c = ext_c[lr:lr + CONV_HALO, :]
    new_p = ext_p[lr:lr + POOL_HALO, :]
    ext_c[0:CONV_HALO, :] = new_c
    ext_p[0:POOL_HALO, :] = new_p

    @pl.when(c == nc - 1)
    def _():
        convn_o[0] = new_c
        pooln_o[0] = new_p
        for g in range(SSM_GROUPS):
            ssmn_o[0, g * GROUP_W:(g + 1) * GROUP_W, :] = ht_scr[g].T


def ssd_pool(proj, n, l, conv_prev, ssm_prev, pool_prev, pos0, p):
    q = SSM_CHUNK
    if l % q == 0:
        lr, nc = q, l // q
    elif l < q and l % SUBLANES == 0:
        lr, nc = l, 1
    else:
        raise ValueError(f"unsupported sequence length {l}")
    convp = jnp.pad(conv_prev, ((0, 0), (CONV_HALO - (SSM_CONV - 1), 0), (0, 0)))
    poolp = jnp.pad(pool_prev, ((0, 0), (POOL_HALO - POOL_HIST, 0), (0, 0)))
    ssmp = ssm_prev.reshape(n, SSM_D_INNER, SSM_D_STATE)
    pad_h = LANES - SSM_HEADS
    tril = jnp.asarray(np.tril(np.ones((q, q), np.float32)), BF16)
    rexp_np = np.zeros((LANES, SSM_D_INNER), np.float32)
    rexp_np[np.arange(SSM_D_INNER) // SSM_HEAD_DIM, np.arange(SSM_D_INNER)] = 1.0
    rexp = jnp.asarray(rexp_np, BF16)
    c2 = lambda b, c: (0, 0)
    seq3 = lambda b, c: (b, 0, 0)
    kern = functools.partial(_ssd_pool_kernel, lr=lr, pos0=pos0)
    mixed, convn, ssmn, pooln = pl.pallas_call(
        kern,
        grid=(n, nc),
        in_specs=[pl.BlockSpec((lr, HYB_PROJ), lambda b, c: (b * nc + c, 0)),
                  pl.BlockSpec((1, CONV_HALO, SSM_CONV_DIM), seq3),
                  pl.BlockSpec((1, SSM_D_INNER, SSM_D_STATE), seq3),
                  pl.BlockSpec((1, POOL_HALO, POOL_DIM), seq3),
                  pl.BlockSpec((SSM_CONV, SSM_CONV_DIM), c2),
                  pl.BlockSpec((1, SSM_CONV_DIM), c2),
                  pl.BlockSpec((1, LANES), c2),
                  pl.BlockSpec((1, LANES), c2),
                  pl.BlockSpec((1, SSM_D_INNER), c2),
                  pl.BlockSpec((1, SSM_D_INNER), c2),
                  pl.BlockSpec((len(POOL_WINDOWS), POOL_GROUP, POOL_GROUP), lambda b, c: (0, 0, 0)),
                  pl.BlockSpec((1, POOL_DIM), c2),
                  pl.BlockSpec((q, q), c2),
                  pl.BlockSpec((LANES, SSM_D_INNER), c2)],
        out_specs=[pl.BlockSpec((lr, HYB_MIX), lambda b, c: (b * nc + c, 0)),
                   pl.BlockSpec((1, CONV_HALO, SSM_CONV_DIM), seq3),
                   pl.BlockSpec((1, SSM_D_INNER, SSM_D_STATE), seq3),
                   pl.BlockSpec((1, POOL_HALO, POOL_DIM), seq3)],
        out_shape=[jax.ShapeDtypeStruct((n * l, HYB_MIX), BF16),
                   jax.ShapeDtypeStruct((n, CONV_HALO, SSM_CONV_DIM), F32),
                   jax.ShapeDtypeStruct((n, SSM_D_INNER, SSM_D_STATE), F32),
                   jax.ShapeDtypeStruct((n, POOL_HALO, POOL_DIM), F32)],
        scratch_shapes=[pltpu.VMEM((CONV_HALO + q, SSM_CONV_DIM), F32),
                        pltpu.VMEM((POOL_HALO + q, POOL_DIM), F32),
                        pltpu.VMEM((SSM_GROUPS, SSM_D_STATE, GROUP_W), F32)],
        compiler_params=_params(("arbitrary", "arbitrary")),
        name="ssd_pool",
    )(proj, convp, ssmp, poolp,
      p["ssm_conv_w"], p["ssm_conv_b"].reshape(1, -1),
      jnp.pad(p["ssm_dt_bias"], (0, pad_h)).reshape(1, LANES),
      jnp.pad(p["ssm_a_log"], (0, pad_h)).reshape(1, LANES),
      jnp.repeat(p["ssm_d"], SSM_HEAD_DIM).reshape(1, SSM_D_INNER),
      p["ssm_norm_g"].reshape(1, SSM_D_INNER),
      p["pool_w"].astype(BF16), p["pool_scale"].reshape(1, POOL_DIM), tril, rexp)
    conv_new = convn[:, CONV_HALO - (SSM_CONV - 1):, :]
    ssm_new = ssmn.reshape(n, SSM_HEADS, SSM_HEAD_DIM, SSM_D_STATE)
    pool_new = pooln[:, POOL_HALO - POOL_HIST:, :]
    return mixed, conv_new, ssm_new, pool_new


def _bucket_bounds():
    max_exact = REL_BUCKETS // 2
    bounds = []
    for b in range(max_exact + 1, REL_BUCKETS):
        d = max_exact
        while True:
            val = math.log(d / max_exact) / math.log(REL_MAX_DIST / max_exact) * (REL_BUCKETS - max_exact)
            if max_exact + int(val) >= b:
                break
            d += 1
        bounds.append(d)
    return bounds


def _bias_kernel(rb_ref, o_ref, *, tiles, rows, cols, sign):
    h = pl.program_id(0)
    max_exact = REL_BUCKETS // 2
    bounds = _bucket_bounds()
    far = rb_ref[REL_BUCKETS - 1, h]
    ii = lax.broadcasted_iota(jnp.int32, (rows, cols), 0)
    jj = lax.broadcasted_iota(jnp.int32, (rows, cols), 1)
    for oi, (kind, off) in enumerate(tiles):
        if kind == "zero":
            o_ref[0, oi] = jnp.zeros((rows, cols), F32)
            continue
        if kind == "neg":
            o_ref[0, oi] = jnp.full((rows, cols), NEG_INF, F32)
            continue
        rel = off + sign * (ii - jj)
        dist = jnp.maximum(rel, 0)
        large = jnp.full((rows, cols), max_exact, jnp.int32)
        for bnd in bounds:
            large = large + (dist >= bnd).astype(jnp.int32)
        bucket = jnp.where(dist < max_exact, dist, large)
        bias = jnp.zeros((rows, cols), F32)
        for b in range(REL_BUCKETS):
            bias = bias + jnp.where(bucket == b, (rb_ref[b, h] - far) * LOG2E, 0.0)
        if kind == "bias_causal":
            bias = jnp.where(rel >= 0, bias, NEG_INF)
        o_ref[0, oi] = bias


def bias_tiles(rel_bias, tiles, rows, cols, sign=1):
    kern = functools.partial(_bias_kernel, tiles=tuple(tiles), rows=rows, cols=cols, sign=sign)
    return pl.pallas_call(
        kern,
        grid=(ATT_HEADS,),
        in_specs=[pl.BlockSpec(memory_space=pltpu.SMEM)],
        out_specs=pl.BlockSpec((1, len(tiles), rows, cols), lambda h: (h, 0, 0, 0)),
        out_shape=jax.ShapeDtypeStruct((ATT_HEADS, len(tiles), rows, cols), F32),
        compiler_params=_params(("arbitrary",)),
        name="bias_tiles",
    )(rel_bias)


def _lambda(lam_ref, lam_init):
    l1 = jnp.sum(lam_ref[0:1, :] * lam_ref[1:2, :], axis=-1, keepdims=True)
    l2 = jnp.sum(lam_ref[2:3, :] * lam_ref[3:4, :], axis=-1, keepdims=True)
    return jnp.exp(l1) - jnp.exp(l2) + lam_init


def _subln(o, sg_ref, lam_init):
    ms = jnp.mean(o * o, axis=-1, keepdims=True)
    return o * lax.rsqrt(ms + EPS) * sg_ref[...] * (1.0 - lam_init)


def _attn_kernel(lam_ref, q_ref, k_ref, vt_ref, bias_ref, sgt_ref, o_ref,
                 m_scr, l_scr, acc_scr, s_a, s_b, p_a, p_b, al_a, al_b, *, tq, lam_init):
    qi = pl.program_id(2)
    q = q_ref[...]
    lane = lax.broadcasted_iota(jnp.int32, (tq, LANES), 1)
    zero = jnp.zeros_like(q)
    q_halves = (jnp.where(lane < ATT_DH, q, zero), jnp.where(lane < ATT_DH, zero, q))
    m_scr[...] = jnp.full(m_scr.shape, NEG_INF, F32)
    l_scr[...] = jnp.zeros(l_scr.shape, F32)
    acc_scr[...] = jnp.zeros(acc_scr.shape, F32)
    p_b[...] = jnp.zeros(p_b.shape, BF16)
    al_b[...] = jnp.ones(al_b.shape, F32)

    def block_offset(kb):
        return pl.multiple_of(jnp.clip(kb, 0, qi) * tq, tq)

    def scores(kb, s_ref):
        k = k_ref[pl.ds(block_offset(kb), tq), :]
        for mi in range(2):
            s_ref[mi] = lax.dot_general(k, q_halves[mi], NT_DIMS, preferred_element_type=F32)

    def softmax(kb, s_ref, p_ref, al_ref):
        bias_t = bias_ref[0, jnp.clip(kb - qi + 2, 0, 3)]
        for mi in range(2):
            s = s_ref[mi] + bias_t
            m_old = m_scr[mi]
            m_new = jnp.maximum(m_old, jnp.max(s, axis=0, keepdims=True))
            alpha = jnp.exp2(m_old - m_new)
            pr = jnp.exp2(s - m_new)
            l_scr[mi] = alpha * l_scr[mi] + jnp.sum(pr, axis=0, keepdims=True)
            m_scr[mi] = m_new
            p_ref[mi] = pr.astype(BF16)
            al_ref[mi] = alpha

    def values(kb, p_ref, al_ref):
        vt = vt_ref[:, pl.ds(block_offset(kb), tq)]
        for mi in range(2):
            acc_scr[mi] = al_ref[mi] * acc_scr[mi] + jnp.dot(vt, p_ref[mi], preferred_element_type=F32)

    def pair(j, carry):
        a = 2 * j
        scores(a + 1, s_b)
        softmax(a, s_a, p_a, al_a)
        values(a - 1, p_b, al_b)
        scores(a + 2, s_a)
        softmax(a + 1, s_b, p_b, al_b)
        values(a, p_a, al_a)
        return carry

    n_pairs = (qi + 2) // 2
    scores(0, s_a)
    lax.fori_loop(0, n_pairs, pair, 0)
    values(2 * n_pairs - 1, p_b, al_b)
    lam = _lambda(lam_ref, lam_init)
    o_t = acc_scr[0] / l_scr[0] - lam * (acc_scr[1] / l_scr[1])
    ms = jnp.mean(o_t * o_t, axis=0, keepdims=True)
    o_t = o_t * lax.rsqrt(ms + EPS) * sgt_ref[...] * (1.0 - lam_init)
    o_ref[...] = o_t.T.astype(BF16)


def diff_attention_prompt(q, k, vt, n, l, bias_t, lam_vecs, subln_g, lam_init, tq):
    nq = l // tq
    kern = functools.partial(_attn_kernel, tq=tq, lam_init=lam_init)
    sg_t = jnp.broadcast_to(subln_g.reshape(ATT_DV, 1), (ATT_DV, tq))
    return pl.pallas_call(
        kern,
        grid=(n, ATT_HEADS, nq),
        in_specs=[pl.BlockSpec((4, ATT_DH), lambda b, h, i: (0, 0)),
                  pl.BlockSpec((tq, LANES), lambda b, h, i: (b * nq + i, h)),
                  pl.BlockSpec((l, LANES), lambda b, h, i: (b, h)),
                  pl.BlockSpec((LANES, l), lambda b, h, i: (h, b)),
                  pl.BlockSpec((1, 4, tq, tq), lambda b, h, i: (h, 0, 0, 0)),
                  pl.BlockSpec((ATT_DV, tq), lambda b, h, i: (0, 0))],
        out_specs=pl.BlockSpec((tq, LANES), lambda b, h, i: (b * nq + i, h)),
        out_shape=jax.ShapeDtypeStruct((n * l, ATT_WIDTH), BF16),
        scratch_shapes=[pltpu.VMEM((2, 1, tq), F32), pltpu.VMEM((2, 1, tq), F32),
                        pltpu.VMEM((2, ATT_DV, tq), F32),
                        pltpu.VMEM((2, tq, tq), F32), pltpu.VMEM((2, tq, tq), F32),
                        pltpu.VMEM((2, tq, tq), BF16), pltpu.VMEM((2, tq, tq), BF16),
                        pltpu.VMEM((2, 1, tq), F32), pltpu.VMEM((2, 1, tq), F32)],
        compiler_params=_params(("arbitrary", "arbitrary", "arbitrary")),
        name="diff_attn_prompt",
    )(lam_vecs, q, k, vt, bias_t, sg_t)


def _attn_decode_kernel(pt_ref, lam_ref, q_ref, kn_ref, vn_ref, pmask_ref, nmask_ref, sg_ref, *rest,
                        pps, l, lam_init):
    k_refs = rest[:pps]
    v_refs = rest[pps:2 * pps]
    o_ref = rest[2 * pps]
    q2_scr, m_scr, l_scr, acc_scr = rest[2 * pps + 1:]
    s_idx = pl.program_id(1)
    ns = pl.num_programs(1)
    last = s_idx == ns - 1

    @pl.when(s_idx == 0)
    def _():
        qf = q_ref[0].astype(F32)
        lane = lax.broadcasted_iota(jnp.int32, (l, LANES), 1)
        pieces = []
        for h in range(ATT_HEADS):
            qh = qf[:, h * LANES:(h + 1) * LANES]
            pieces += [jnp.where(lane < ATT_DH, qh, 0.0), jnp.where(lane < ATT_DH, 0.0, qh)]
        q2_scr[...] = jnp.concatenate(pieces, axis=0).astype(BF16)
        m_scr[...] = jnp.full(m_scr.shape, NEG_INF, F32)
        l_scr[...] = jnp.zeros(l_scr.shape, F32)
        acc_scr[...] = jnp.zeros(acc_scr.shape, F32)

    def update(s, vb):
        m_old = m_scr[...]
        m_new = jnp.maximum(m_old, jnp.max(s, axis=-1, keepdims=True))
        alpha = jnp.exp2(m_old - m_new)
        pr = jnp.exp2(s - m_new)
        l_scr[...] = alpha * l_scr[...] + jnp.sum(pr, axis=-1, keepdims=True)
        acc_scr[...] = alpha * acc_scr[...] + jnp.dot(pr.astype(BF16), vb, preferred_element_type=F32)
        m_scr[...] = m_new

    q2 = q2_scr[...]
    s_parts = []
    for pi in range(pps):
        kb = k_refs[pi][0].astype(BF16)
        s = lax.dot_general(q2, kb, NT_DIMS, preferred_element_type=F32)
        mask = jnp.where(last, pmask_ref[1], pmask_ref[0]) if pi == pps - 1 else pmask_ref[0]
        s_parts.append(s + mask)
    vb = jnp.concatenate([vr[0].astype(BF16) for vr in v_refs], axis=0)
    update(jnp.concatenate(s_parts, axis=1), vb)

    @pl.when(last)
    def _():
        nk = kn_ref.shape[1]
        pad = jnp.zeros((PAGE - nk, LANES), F32)
        kn = jnp.concatenate([kn_ref[0], pad], axis=0).astype(BF16)
        vn = jnp.concatenate([vn_ref[0], pad], axis=0).astype(BF16)
        sn = lax.dot_general(q2, kn, NT_DIMS, preferred_element_type=F32) + nmask_ref[...]
        update(sn, vn)
        lam = _lambda(lam_ref, lam_init)
        outs = []
        for h in range(ATT_HEADS):
            r0 = 2 * h * l
            o0 = acc_scr[r0:r0 + l, :] / l_scr[r0:r0 + l, :]
            o1 = acc_scr[r0 + l:r0 + 2 * l, :] / l_scr[r0 + l:r0 + 2 * l, :]
            outs.append(_subln(o0 - lam * o1, sg_ref, lam_init))
        o_ref[0] = jnp.concatenate(outs, axis=1).astype(BF16)


def diff_attention_decode(q, k_new, v_new, cache_k, cache_v, page_table, pmask, nmask, lam_vecs, subln_g, lam_init):
    n, l, _ = q.shape
    n_pages = page_table.shape[1]
    pps = next(c for c in (DECODE_PAGES_PER_STEP, 4, 2, 1) if n_pages % c == 0)
    ns = n_pages // pps
    rows = 2 * ATT_HEADS * l
    pk = PAGE * ATT_HEADS
    kern = functools.partial(_attn_decode_kernel, pps=pps, l=l, lam_init=lam_init)

    def page_spec(pi):
        return pl.BlockSpec((1, pk, LANES), lambda b, s, pt: (pt[b * n_pages + s * pps + pi], 0, 0))

    seq = lambda b, s, pt: (b, 0, 0)
    grid_spec = pltpu.PrefetchScalarGridSpec(
        num_scalar_prefetch=1,
        grid=(n, ns),
        in_specs=[pl.BlockSpec((4, ATT_DH), lambda b, s, pt: (0, 0)),
                  pl.BlockSpec((1, l, ATT_WIDTH), seq),
                  pl.BlockSpec((1, l * ATT_HEADS, LANES), seq),
                  pl.BlockSpec((1, l * ATT_HEADS, LANES), seq),
                  pl.BlockSpec((2, rows, pk), lambda b, s, pt: (0, 0, 0)),
                  pl.BlockSpec((rows, PAGE), lambda b, s, pt: (0, 0)),
                  pl.BlockSpec((1, ATT_DV), lambda b, s, pt: (0, 0))]
                 + [page_spec(pi) for pi in range(pps)] * 2,
        out_specs=pl.BlockSpec((1, l, ATT_WIDTH), seq),
        scratch_shapes=[pltpu.VMEM((rows, LANES), BF16), pltpu.VMEM((rows, 1), F32),
                        pltpu.VMEM((rows, 1), F32), pltpu.VMEM((rows, ATT_DV), F32)])
    return pl.pallas_call(
        kern,
        grid_spec=grid_spec,
        out_shape=jax.ShapeDtypeStruct((n, l, ATT_WIDTH), BF16),
        compiler_params=_params(("arbitrary", "arbitrary")),
        name="diff_attn_decode",
    )(page_table.reshape(-1), lam_vecs, q, k_new, v_new, pmask, nmask, subln_g.reshape(1, ATT_DV),
      *([cache_k] * pps), *([cache_v] * pps))


def decode_masks(bt, l):
    rows = 2 * ATT_HEADS * l
    r = np.arange(rows)
    row_head, row_tok = r // (2 * l), r % l
    c = np.arange(PAGE * ATT_HEADS)
    head_ok = (c[None, :] % ATT_HEADS) == row_head[:, None]
    neg = np.where(head_ok, 0.0, -np.inf).astype(np.float32)
    cn = np.arange(PAGE)
    key_n = cn // ATT_HEADS
    new_ok = ((cn[None, :] % ATT_HEADS) == row_head[:, None]) & (key_n[None, :] <= row_tok[:, None]) \
        & (key_n[None, :] < l)
    neg_n = np.where(new_ok, 0.0, -np.inf).astype(np.float32)
    per_row = jnp.broadcast_to(bt[:, :, None], (ATT_HEADS, 2, 2, l, PAGE))
    last = jnp.repeat(per_row[:, 0].reshape(rows, PAGE), ATT_HEADS, axis=1)
    new = jnp.repeat(per_row[:, 1].reshape(rows, PAGE)[:, :PAGE // ATT_HEADS], ATT_HEADS, axis=1)
    pmask = jnp.stack([jnp.asarray(neg), last + neg])
    nmask = new + neg_n
    return pmask, nmask


_PEER_CAND = [(i, j) for i in range(PEER_NTOP) for j in range(PEER_NTOP) if (i + 1) * (j + 1) <= PEER_NTOP]


def _oddeven_merge(lo, hi, r):
    step = r * 2
    if step < hi - lo:
        yield from _oddeven_merge(lo, hi, step)
        yield from _oddeven_merge(lo + r, hi, step)
        yield from [(i, i + r) for i in range(lo + r, hi - r, step)]
    else:
        yield (lo, lo + r)


def _oddeven_merge_sort(lo, hi):
    if hi - lo >= 1:
        mid = lo + (hi - lo) // 2
        yield from _oddeven_merge_sort(lo, mid)
        yield from _oddeven_merge_sort(mid + 1, hi)
        yield from _oddeven_merge(lo, hi, 1)


_SORT16 = tuple(_oddeven_merge_sort(0, PEER_NKEYS // SUBLANES - 1))


def _route_kernel(hn_ref, wq_ref, sk_ref, r2_o, e2_o, n1_o, e1_o, q_scr, s_scr, a_scr, b_scr):
    q_scr[...] = jnp.dot(hn_ref[...], wq_ref[...], preferred_element_type=F32)
    groups = PEER_NKEYS // SUBLANES
    for h in range(PEER_HEADS):
        for m in range(2):
            c0 = (2 * h + m) * LANES
            qs = q_scr[:, c0:c0 + LANES].astype(BF16)
            s = lax.dot_general(sk_ref[m], qs, NT_DIMS, preferred_element_type=F32)
            s_scr[m, h] = s
            top_scr = a_scr if m == 0 else b_scr
            col = [s[g * SUBLANES:(g + 1) * SUBLANES, :] for g in range(groups)]
            for (i, j) in _SORT16:
                col[i], col[j] = jnp.maximum(col[i], col[j]), jnp.minimum(col[i], col[j])
            for r in range(PEER_NTOP):
                mx = jnp.max(col[0], axis=0, keepdims=True)
                top_scr[r, h:h + 1, :] = mx
                left = PEER_NTOP - 1 - r
                if left:
                    hit = col[0] == mx
                    depth = min(left, groups)
                    for g in range(depth):
                        below = col[g + 1] if g + 1 < groups else NEG_INF
                        col[g] = jnp.where(hit, below, col[g])
            if m == 1:
                rank = jnp.full(s.shape, float(PEER_NTOP), F32)
                for b in reversed(range(PEER_NTOP)):
                    rank = jnp.where(s >= b_scr[b, h:h + 1, :], float(b), rank)
                r2_o[h] = rank.astype(BF16)
    a_top = [a_scr[r] for r in range(PEER_NTOP)]
    b_top = [b_scr[r] for r in range(PEER_NTOP)]
    cands = [a_top[i] + b_top[j] for (i, j) in _PEER_CAND]
    xs = list(cands)
    ranked = []
    for r in range(PEER_NTOP):
        mx = functools.reduce(jnp.maximum, xs)
        ranked.append(mx)
        if r + 1 < PEER_NTOP:
            xs = [jnp.where(x == mx, NEG_INF, x) for x in xs]
    thr = 0.5 * (ranked[PEER_TOPK - 1] + ranked[PEER_TOPK])
    top = a_top[0] + b_top[0]
    z = functools.reduce(lambda u, w: u + w, [jnp.where(c >= thr, jnp.exp(c - top), 0.0) for c in cands])
    scale = 1.0 / z
    for h in range(PEER_HEADS):
        s1 = s_scr[0, h]
        s2 = s_scr[1, h]
        a0 = a_top[0][h:h + 1]
        b0 = b_top[0][h:h + 1]
        e2_o[h] = jnp.exp(s2 - b0).astype(BF16)
        need = thr[h:h + 1] - s1
        n1 = jnp.zeros(s1.shape, F32)
        for b in range(PEER_NTOP):
            n1 = jnp.where(b_top[b][h:h + 1] >= need, float(b + 1), n1)
        n1_o[h] = n1
        e1_o[h] = jnp.exp(s1 - a0) * scale[h:h + 1]


def peer_route(hn, wq, sk):
    t, d = hn.shape
    tm = min(256, t)
    spec = pl.BlockSpec((PEER_HEADS, PEER_NKEYS, tm), lambda i: (0, 0, i))
    shape = lambda dt: jax.ShapeDtypeStruct((PEER_HEADS, PEER_NKEYS, t), dt)
    return pl.pallas_call(
        _route_kernel,
        grid=(t // tm,),
        in_specs=[pl.BlockSpec((tm, d), lambda i: (i, 0)),
                  pl.BlockSpec(wq.shape, lambda i: (0, 0)),
                  pl.BlockSpec(sk.shape, lambda i: (0, 0, 0))],
        out_specs=[spec, spec, spec, spec],
        out_shape=[shape(BF16), shape(BF16), shape(F32), shape(F32)],
        scratch_shapes=[pltpu.VMEM((tm, wq.shape[1]), F32),
                        pltpu.VMEM((2, PEER_HEADS, PEER_NKEYS, tm), F32),
                        pltpu.VMEM((PEER_NTOP, PEER_HEADS, tm), F32),
                        pltpu.VMEM((PEER_NTOP, PEER_HEADS, tm), F32)],
        compiler_params=_params(("arbitrary",)),
        name="peer_route",
    )(hn, wq, sk)


GELU_C0 = math.sqrt(2.0 / math.pi)
GELU_C1 = GELU_C0 * 0.044715
GELU_K0 = -2.0 * GELU_C0 * LOG2E
GELU_K1 = -2.0 * GELU_C1 * LOG2E


def _peer_kernel(hn_ref, u_ref, vt_ref, r2_ref, e2_ref, n1_ref, e1_ref, x1_ref, g2_ref, o_ref,
                 acc_scr, act_scr, wa_scr):
    e = pl.program_id(2)
    tm = hn_ref.shape[0]

    @pl.when(e == 0)
    def _():
        acc_scr[...] = jnp.zeros(acc_scr.shape, F32)

    act_scr[...] = lax.dot_general(u_ref[...], hn_ref[...], NT_DIMS, preferred_element_type=F32)
    for il in range(PEER_IB):
        rows = slice(il * PEER_NKEYS, (il + 1) * PEER_NKEYS)
        w = None
        for h in range(PEER_HEADS):
            n1 = jnp.broadcast_to(n1_ref[h, il:il + 1, :], (PEER_NKEYS, tm)).astype(BF16)
            e1 = jnp.broadcast_to(e1_ref[h, il:il + 1, :], (PEER_NKEYS, tm)).astype(BF16)
            e2 = e2_ref[h]
            term = jnp.where(r2_ref[h] < n1, e2 * e1, jnp.zeros_like(e2))
            w = term if w is None else w + term
        x = act_scr[rows, :].astype(BF16)
        decay = jnp.exp2(x * (GELU_K0 + GELU_K1 * (x * x)))
        wa_scr[rows, :] = w * (x / (1.0 + decay))
    acc_scr[...] += jnp.dot(vt_ref[...], wa_scr[...], preferred_element_type=F32)

    @pl.when(e == pl.num_programs(2) - 1)
    def _():
        y = acc_scr[...].T
        o_ref[...] = x1_ref[...] + g2_ref[...] * y.reshape(x1_ref.shape)


def peer_mix(hn, u_b, vt_b, r2t, e2t, n1t, e1t, x1, g2):
    n, l, d = x1.shape
    nb, lb = _row_blocks(n, l)
    tm = nb * lb
    lk = l // lb
    ne = PEER_EXPERTS // PEER_EB
    tok = lambda i, k, e: (i * lk + k, 0)
    keys2 = pl.BlockSpec((PEER_HEADS, PEER_NKEYS, tm), lambda i, k, e: (0, 0, i * lk + k))
    keys1 = pl.BlockSpec((PEER_HEADS, PEER_IB, tm), lambda i, k, e: (0, e, i * lk + k))
    return pl.pallas_call(
        _peer_kernel,
        grid=(n // nb, lk, ne),
        in_specs=[pl.BlockSpec((tm, d), tok),
                  pl.BlockSpec((PEER_EB, d), lambda i, k, e: (e, 0)),
                  pl.BlockSpec((d, PEER_EB), lambda i, k, e: (0, e)),
                  keys2, keys2, keys1, keys1,
                  pl.BlockSpec((nb, lb, d), lambda i, k, e: (i, k, 0)),
                  pl.BlockSpec((nb, 1, d), lambda i, k, e: (i, 0, 0))],
        out_specs=pl.BlockSpec((nb, lb, d), lambda i, k, e: (i, k, 0)),
        out_shape=jax.ShapeDtypeStruct((n, l, d), F32),
        scratch_shapes=[pltpu.VMEM((d, tm), F32), pltpu.VMEM((PEER_EB, tm), F32),
                        pltpu.VMEM((PEER_EB, tm), BF16)],
        compiler_params=_params(("arbitrary", "arbitrary", "arbitrary")),
        name="peer_mix",
    )(hn, u_b, vt_b, r2t, e2t, n1t, e1t, x1, g2)


def peer_ffn_residual(hn, x1, g2, wq_b, sk_b, u_b, vt_b):
    r2t, e2t, n1t, e1t = peer_route(hn, wq_b, sk_b)
    return peer_mix(hn, u_b, vt_b, r2t, e2t, n1t, e1t, x1, g2)


def _trunk(x, mods, pos0, conv_prev, ssm_prev, pool_prev, paged_kv, p, w):
    n, l, d = x.shape
    t = n * l
    sh1, sc1, g1, sh2, sc2, g2 = mods[0]
    proj = norm_inproj(x, p["norm_mix_g"][0], sc1, sh1, w["hyb_in"], tn=HYB_PROJ // 3)
    mixed, conv_new, ssm_new, pool_new = ssd_pool(proj, n, l, conv_prev, ssm_prev, pool_prev, pos0, w["ssm"])
    x1, hn = outproj_residual(mixed, w["hyb_out"], x, g1, p["norm_ffn_g"][0], sc2, sh2)
    x2 = peer_ffn_residual(hn, x1, g2, w["peer_wq"][0], w["peer_sk"][0], w["peer_u"][0], w["peer_vt"][0])

    sh1, sc1, g1, sh2, sc2, g2 = mods[1]
    qb, kf, kb, vf, vt = norm_qkv(x2, p["norm_mix_g"][1], sc1, sh1, w["att_in"], p["att_q_g"][0], p["att_k_g"][0])
    lam_init = 0.8 - 0.6 * math.exp(-0.3 * 1)
    lam_vecs = jnp.stack([p["att_lambda_q1"][0], p["att_lambda_k1"][0],
                          p["att_lambda_q2"][0], p["att_lambda_k2"][0]])
    if paged_kv is None:
        tq = min(ATT_TQ, l)
        bias_t = bias_tiles(p["rel_bias"], (("zero", 0), ("bias", tq), ("bias_causal", 0), ("neg", 0)),
                            tq, tq, sign=-1)
        o = diff_attention_prompt(qb, kb, vt, n, l, bias_t, lam_vecs, p["att_subln_g"][0], lam_init, tq)
    else:
        cache_k, cache_v, page_table = paged_kv
        past_len = page_table.shape[1] * PAGE
        bt = bias_tiles(p["rel_bias"], (("bias", pos0 - (past_len - PAGE)), ("bias", pos0 - past_len)),
                        l, PAGE)
        pmask, nmask = decode_masks(bt, l)
        o = diff_attention_decode(qb.reshape(n, l, ATT_WIDTH), kf.reshape(n, l * ATT_HEADS, ATT_DV),
                                  vf.reshape(n, l * ATT_HEADS, ATT_DV), cache_k, cache_v, page_table,
                                  pmask, nmask, lam_vecs, p["att_subln_g"][0], lam_init).reshape(t, ATT_WIDTH)
    x3, hn = outproj_residual(o, w["att_out"], x2, g1, p["norm_ffn_g"][1], sc2, sh2)
    x4 = peer_ffn_residual(hn, x3, g2, w["peer_wq"][1], w["peer_sk"][1], w["peer_u"][1], w["peer_vt"][1])
    k_new = kf.reshape(1, n, l, ATT_HEADS, 2 * ATT_DH)
    v_new = vf.reshape(1, n, l, ATT_HEADS, ATT_DV)
    return x4, conv_new[None], ssm_new[None], pool_new[None], k_new, v_new


def kernel(x_prompt, x_sample, state_conv, state_ssm, state_pool, cache_k, cache_v, page_table,
           c_prompt, c_sample, rel_bias, norm_mix_g, norm_ffn_g, ada_w, ada_b,
           hyb_w_in, ssm_conv_w, ssm_conv_b, ssm_dt_bias, ssm_a_log, ssm_d, ssm_norm_g,
           pool_w, pool_scale, hyb_w_out, att_w_in, att_q_g, att_k_g,
           att_lambda_q1, att_lambda_k1, att_lambda_q2, att_lambda_k2, att_subln_g, att_w_out,
           peer_wq, peer_subkeys, peer_u, peer_v):
    p = {
        "rel_bias": rel_bias, "norm_mix_g": norm_mix_g, "norm_ffn_g": norm_ffn_g,
        "att_q_g": att_q_g, "att_k_g": att_k_g,
        "att_lambda_q1": att_lambda_q1, "att_lambda_k1": att_lambda_k1,
        "att_lambda_q2": att_lambda_q2, "att_lambda_k2": att_lambda_k2, "att_subln_g": att_subln_g,
    }
    w_in = hyb_w_in[0]
    z_w = w_in[:, :SSM_D_INNER]
    xbc_w = w_in[:, SSM_D_INNER:SSM_D_INNER + SSM_CONV_DIM]
    dt_w = w_in[:, SSM_D_INNER + SSM_CONV_DIM:SSM_D_INNER + SSM_CONV_DIM + SSM_HEADS]
    u_w = w_in[:, SSM_D_INNER + SSM_CONV_DIM + SSM_HEADS:]
    hyb_in = jnp.concatenate([z_w, xbc_w, u_w, jnp.pad(dt_w, ((0, 0), (0, LANES - SSM_HEADS)))], axis=1)
    w = {
        "hyb_in": hyb_in.astype(BF16),
        "hyb_out": hyb_w_out[0].astype(BF16),
        "att_in": att_w_in[0].astype(BF16),
        "att_out": att_w_out[0].astype(BF16),
        "peer_wq": peer_wq.astype(BF16),
        "peer_sk": peer_subkeys.astype(BF16),
        "peer_u": peer_u.astype(BF16),
        "peer_vt": jnp.swapaxes(peer_v, 1, 2).astype(BF16),
        "ssm": {"ssm_conv_w": ssm_conv_w[0], "ssm_conv_b": ssm_conv_b[0], "ssm_dt_bias": ssm_dt_bias[0],
                "ssm_a_log": ssm_a_log[0], "ssm_d": ssm_d[0], "ssm_norm_g": ssm_norm_g[0],
                "pool_w": pool_w[0], "pool_scale": pool_scale[0]},
    }
    nb, ns = x_prompt.shape[0], x_sample.shape[0]
    d = x_prompt.shape[2]
    rows = nb + ns
    rpad = -rows % SUBLANES
    c_all = jnp.concatenate([c_prompt, c_sample, jnp.zeros((rpad, d), F32)], axis=0)
    mod = ada_mod(c_all, ada_w, ada_b)

    def mods_for(lo, cnt):
        return [tuple(mod[i, lo:lo + cnt, k * d:(k + 1) * d].reshape(cnt, 1, d) for k in range(6))
                for i in range(mod.shape[0])]

    conv0 = jnp.zeros((nb, SSM_CONV - 1, SSM_CONV_DIM), F32)
    ssm0 = jnp.zeros((nb, SSM_HEADS, SSM_HEAD_DIM, SSM_D_STATE), F32)
    pool0 = jnp.zeros((nb, POOL_HIST, POOL_DIM), F32)
    yp, conv_p, ssm_p, pool_p, k_p, v_p = _trunk(x_prompt, mods_for(0, nb), 0, conv0, ssm0, pool0, None, p, w)
    past_len = page_table.shape[1] * cache_k.shape[2]
    n_phys = cache_k.shape[1]
    ys, conv_s, ssm_s, pool_s, k_s, v_s = _trunk(
        x_sample, mods_for(nb, ns), past_len, state_conv[0], state_ssm[0], state_pool[0],
        (cache_k[0].reshape(n_phys, PAGE * ATT_HEADS, ATT_DV), cache_v[0].reshape(n_phys, PAGE * ATT_HEADS, ATT_DV),
         page_table),
        p, w)
    return (yp, ys, conv_p, ssm_p, pool_p, k_p, v_p, conv_s, ssm_s, pool_s, k_s, v_s)
```

```python
import functools
import math

import numpy as np
import jax
import jax.numpy as jnp
from jax import lax
from jax.experimental import pallas as pl
from jax.experimental.pallas import tpu as pltpu

F32 = jnp.float32
BF16 = jnp.bfloat16

D_MODEL = 1024
EPS = 1e-6
LANES = 128
SUBLANES = 8

SSM_D_INNER = 2048
SSM_HEAD_DIM = 64
SSM_HEADS = 32
SSM_GROUPS = 2
SSM_D_STATE = 128
SSM_CONV = 4
SSM_CHUNK = 128
SSM_CONV_DIM = 2560
GROUP_W = SSM_D_INNER // SSM_GROUPS

POOL_WINDOWS = (2, 4, 8, 16)
POOL_DIM = 1024
POOL_GROUP = 256
POOL_HIST = 15
POOL_HALO = 16
CONV_HALO = 8

HYB_PROJ = SSM_D_INNER + SSM_CONV_DIM + POOL_DIM + LANES
HYB_MIX = SSM_D_INNER + POOL_DIM

ATT_HEADS = 8
ATT_DH = 64
ATT_DV = 128
ATT_WIDTH = 1024
REL_BUCKETS = 32
REL_MAX_DIST = 128
PAGE = 128
ATT_TQ = 512
DECODE_PAGES_PER_STEP = 16

PEER_HEADS = 8
PEER_NKEYS = 128
PEER_EXPERTS = PEER_NKEYS * PEER_NKEYS
PEER_TOPK = 16
PEER_NTOP = PEER_TOPK + 1
PEER_EB = 2048
PEER_IB = PEER_EB // PEER_NKEYS

NEG_INF = float("-inf")
LOG2E = math.log2(math.e)
NT_DIMS = (((1,), (1,)), ((), ()))

VMEM_LIMIT = 56 * 1024 * 1024


def _params(sem):
    return pltpu.CompilerParams(dimension_semantics=sem, vmem_limit_bytes=VMEM_LIMIT)


def _sigmoid(x):
    return 1.0 / (1.0 + jnp.exp(-x))


def _silu(x):
    return x * _sigmoid(x)


def _split3(x):
    hi = x.astype(BF16)
    r1 = x - hi.astype(F32)
    mid = r1.astype(BF16)
    lo = (r1 - mid.astype(F32)).astype(BF16)
    return hi, mid, lo


def _norm_mod(x3, g_ref, sc_ref, sh_ref):
    ms = jnp.mean(x3 * x3, axis=-1, keepdims=True)
    y = x3 * lax.rsqrt(ms + EPS) * g_ref[...]
    return y * (1.0 + sc_ref[...]) + sh_ref[...]


def _row_blocks(n, l):
    if l >= 512:
        return 1, 512
    if l >= SUBLANES and l % SUBLANES == 0 and l < 512:
        if l == SUBLANES:
            nb = min(n, 64)
            while n % nb:
                nb //= 2
            return nb, l
        return 1, l
    raise ValueError(f"unsupported sequence length {l}")


def _ada_kernel(c_ref, w_ref, b_ref, o_ref):
    ca = _silu(c_ref[...]).astype(BF16)
    o_ref[0] = jnp.dot(ca, w_ref[0].astype(BF16), preferred_element_type=F32) + b_ref[0]


def ada_mod(c_all, ada_w, ada_b):
    r = c_all.shape[0]
    depth, d, n6 = ada_w.shape
    tn = 512
    return pl.pallas_call(
        _ada_kernel,
        grid=(depth, n6 // tn),
        in_specs=[pl.BlockSpec((r, d), lambda i, j: (0, 0)),
                  pl.BlockSpec((1, d, tn), lambda i, j: (i, 0, j)),
                  pl.BlockSpec((1, 1, tn), lambda i, j: (i, 0, j))],
        out_specs=pl.BlockSpec((1, r, tn), lambda i, j: (i, 0, j)),
        out_shape=jax.ShapeDtypeStruct((depth, r, n6), F32),
        compiler_params=_params(("arbitrary", "arbitrary")),
        name="ada_mod",
    )(c_all, ada_w, ada_b.reshape(depth, 1, n6))


def _inproj_kernel(x_ref, g_ref, sc_ref, sh_ref, w_ref, o_ref, h_scr):
    @pl.when(pl.program_id(2) == 0)
    def _():
        h = _norm_mod(x_ref[...], g_ref, sc_ref, sh_ref)
        h_scr[...] = h.reshape(h_scr.shape).astype(BF16)

    o_ref[...] = jnp.dot(h_scr[...], w_ref[...], preferred_element_type=F32)


def norm_inproj(x, g, sc, sh, w, tn):
    n, l, d = x.shape
    nb, lb = _row_blocks(n, l)
    m = nb * lb
    lk = l // lb
    nout = w.shape[1]
    return pl.pallas_call(
        _inproj_kernel,
        grid=(n // nb, lk, nout // tn),
        in_specs=[pl.BlockSpec((nb, lb, d), lambda i, k, j: (i, k, 0)),
                  pl.BlockSpec((1, d), lambda i, k, j: (0, 0)),
                  pl.BlockSpec((nb, 1, d), lambda i, k, j: (i, 0, 0)),
                  pl.BlockSpec((nb, 1, d), lambda i, k, j: (i, 0, 0)),
                  pl.BlockSpec((d, tn), lambda i, k, j: (0, j))],
        out_specs=pl.BlockSpec((m, tn), lambda i, k, j: (i * lk + k, j)),
        out_shape=jax.ShapeDtypeStruct((n * l, nout), F32),
        scratch_shapes=[pltpu.VMEM((m, d), BF16)],
        compiler_params=_params(("arbitrary", "arbitrary", "arbitrary")),
        name="norm_inproj",
    )(x, g.reshape(1, d), sc, sh, w)


def _seg_rms(y, g128, bd):
    sq = y * y
    hi = sq.astype(BF16)
    lo = (sq - hi.astype(F32)).astype(BF16)
    outs = []
    for cb in range(y.shape[1] // LANES):
        sl = slice(cb * LANES, (cb + 1) * LANES)
        ms = (jnp.dot(hi[:, sl], bd, preferred_element_type=F32)
              + jnp.dot(lo[:, sl], bd, preferred_element_type=F32))
        outs.append(y[:, sl] * lax.rsqrt(ms + EPS) * g128)
    return jnp.concatenate(outs, axis=1)


def _qkv_kernel(x_ref, g_ref, sc_ref, sh_ref, w_ref, qg_ref, kg_ref, bd_ref,
                q_o, kf_o, kb_o, vf_o, vt_o, h_scr):
    j = pl.program_id(2)

    @pl.when(j == 0)
    def _():
        h = _norm_mod(x_ref[...], g_ref, sc_ref, sh_ref)
        h_scr[...] = h.reshape(h_scr.shape).astype(BF16)

    y = jnp.dot(h_scr[...], w_ref[...], preferred_element_type=F32)

    @pl.when(j == 0)
    def _():
        q_o[...] = (_seg_rms(y, qg_ref[...], bd_ref[...]) * (ATT_DH ** -0.5 * LOG2E)).astype(BF16)

    @pl.when(j == 1)
    def _():
        kn = _seg_rms(y, kg_ref[...], bd_ref[...])
        kf_o[...] = kn
        kb_o[...] = kn.astype(BF16)

    @pl.when(j == 2)
    def _():
        vf_o[...] = y
        vt_o[...] = y.T.astype(BF16)


def norm_qkv(x, g, sc, sh, w, q_g, k_g):
    n, l, d = x.shape
    nb, lb = _row_blocks(n, l)
    m = nb * lb
    lk = l // lb
    t = n * l
    seg = np.arange(LANES) // ATT_DH
    bd = jnp.asarray((seg[:, None] == seg[None, :]).astype(np.float32) / ATT_DH, BF16)
    row = lambda i, k, j: (i * lk + k, 0)
    const2 = lambda i, k, j: (0, 0)
    outs = pl.pallas_call(
        _qkv_kernel,
        grid=(n // nb, lk, 3),
        in_specs=[pl.BlockSpec((nb, lb, d), lambda i, k, j: (i, k, 0)),
                  pl.BlockSpec((1, d), const2),
                  pl.BlockSpec((nb, 1, d), lambda i, k, j: (i, 0, 0)),
                  pl.BlockSpec((nb, 1, d), lambda i, k, j: (i, 0, 0)),
                  pl.BlockSpec((d, ATT_WIDTH), lambda i, k, j: (0, j)),
                  pl.BlockSpec((1, LANES), const2),
                  pl.BlockSpec((1, LANES), const2),
                  pl.BlockSpec((LANES, LANES), const2)],
        out_specs=[pl.BlockSpec((m, ATT_WIDTH), row)] * 4
                  + [pl.BlockSpec((ATT_WIDTH, m), lambda i, k, j: (0, i * lk + k))],
        out_shape=[jax.ShapeDtypeStruct((t, ATT_WIDTH), BF16),
                   jax.ShapeDtypeStruct((t, ATT_WIDTH), F32),
                   jax.ShapeDtypeStruct((t, ATT_WIDTH), BF16),
                   jax.ShapeDtypeStruct((t, ATT_WIDTH), F32),
                   jax.ShapeDtypeStruct((ATT_WIDTH, t), BF16)],
        scratch_shapes=[pltpu.VMEM((m, d), BF16)],
        compiler_params=_params(("arbitrary", "arbitrary", "arbitrary")),
        name="norm_qkv",
    )(x, g.reshape(1, d), sc, sh, w,
      jnp.tile(q_g, 2).reshape(1, LANES), jnp.tile(k_g, 2).reshape(1, LANES), bd)
    return outs


def _outproj_kernel(a_ref, w_ref, x_ref, g1_ref, gn_ref, sc_ref, sh_ref, x1_o, hn_o):
    y = jnp.dot(a_ref[...], w_ref[...], preferred_element_type=F32)
    x1 = x_ref[...] + g1_ref[...] * y.reshape(x_ref.shape)
    x1_o[...] = x1
    hn_o[...] = _norm_mod(x1, gn_ref, sc_ref, sh_ref).reshape(hn_o.shape).astype(BF16)


def outproj_residual(a, w, x, g1, gn, sc, sh):
    n, l, d = x.shape
    nb, lb = _row_blocks(n, l)
    m = nb * lb
    lk = l // lb
    kdim = a.shape[1]
    seq = lambda i, k: (i, 0, 0)
    return pl.pallas_call(
        _outproj_kernel,
        grid=(n // nb, lk),
        in_specs=[pl.BlockSpec((m, kdim), lambda i, k: (i * lk + k, 0)),
                  pl.BlockSpec((kdim, d), lambda i, k: (0, 0)),
                  pl.BlockSpec((nb, lb, d), lambda i, k: (i, k, 0)),
                  pl.BlockSpec((nb, 1, d), seq),
                  pl.BlockSpec((1, d), lambda i, k: (0, 0)),
                  pl.BlockSpec((nb, 1, d), seq),
                  pl.BlockSpec((nb, 1, d), seq)],
        out_specs=[pl.BlockSpec((nb, lb, d), lambda i, k: (i, k, 0)),
                   pl.BlockSpec((m, d), lambda i, k: (i * lk + k, 0))],
        out_shape=[jax.ShapeDtypeStruct((n, l, d), F32),
                   jax.ShapeDtypeStruct((n * l, d), BF16)],
        compiler_params=_params(("arbitrary", "arbitrary")),
        name="outproj_residual",
    )(a, w, x, g1, gn.reshape(1, d), sc, sh)


def _ssd_pool_kernel(proj_ref, convp_ref, ssmp_ref, poolp_ref, convw_ref, convb_ref, dtb_ref, alog_ref,
                     dexp_ref, normg_ref, poolw_ref, pools_ref, tril_ref, rexp_ref,
                     mixed_o, convn_o, ssmn_o, pooln_o,
                     ext_c, ext_p, ht_scr, *, lr, pos0):
    q = SSM_CHUNK
    c = pl.program_id(1)
    nc = pl.num_programs(1)

    @pl.when(c == 0)
    def _():
        ext_c[0:CONV_HALO, :] = convp_ref[0]
        ext_p[0:POOL_HALO, :] = poolp_ref[0]
        for g in range(SSM_GROUPS):
            ht_scr[g] = ssmp_ref[0, g * GROUP_W:(g + 1) * GROUP_W, :].T

    xbc_cols = slice(SSM_D_INNER, SSM_D_INNER + SSM_CONV_DIM)
    u_cols = slice(SSM_D_INNER + SSM_CONV_DIM, SSM_D_INNER + SSM_CONV_DIM + POOL_DIM)
    dt_cols = slice(SSM_D_INNER + SSM_CONV_DIM + POOL_DIM, HYB_PROJ)

    ext_c[CONV_HALO:CONV_HALO + lr, :] = proj_ref[:, xbc_cols]
    ext_p[POOL_HALO:POOL_HALO + lr, :] = proj_ref[:, u_cols]
    if lr < q:
        ext_c[CONV_HALO + lr:CONV_HALO + q, :] = jnp.zeros((q - lr, SSM_CONV_DIM), F32)
        ext_p[POOL_HALO + lr:POOL_HALO + q, :] = jnp.zeros((q - lr, POOL_DIM), F32)

    yr = q if lr == q else max(lr, 2 * SUBLANES)
    acc = convb_ref[...] + ext_c[pl.ds(CONV_HALO - 3, yr), :] * convw_ref[0:1, :]
    for tap in range(1, SSM_CONV):
        acc = acc + ext_c[pl.ds(CONV_HALO - 3 + tap, yr), :] * convw_ref[tap:tap + 1, :]
    xbc = _silu(acc)
    if yr < q:
        xbc = jnp.concatenate([xbc, jnp.zeros((q - yr, SSM_CONV_DIM), F32)], axis=0)
    xs = xbc[:, :SSM_D_INNER]

    row = lax.broadcasted_iota(jnp.int32, (q, LANES), 0)
    dt_raw = proj_ref[:, dt_cols] + dtb_ref[...]
    if lr < q:
        dt_raw = jnp.concatenate([dt_raw, jnp.zeros((q - lr, LANES), F32)], axis=0)
    dt = jnp.maximum(dt_raw, 0.0) + jnp.log(1.0 + jnp.exp(-jnp.abs(dt_raw)))
    dt = jnp.where(row < lr, dt, 0.0)
    a_neg = -jnp.exp(alog_ref[...])
    d_a = dt * a_neg

    tril = tril_ref[...]
    rexp = rexp_ref[...]
    a_cs = sum(jnp.dot(tril, p, preferred_element_type=F32) for p in _split3(d_a))
    acs_e = sum(jnp.dot(p, rexp, preferred_element_type=F32) for p in _split3(a_cs))
    dt_e = sum(jnp.dot(p, rexp, preferred_element_type=F32) for p in _split3(dt))
    a_cs_t = a_cs.T

    xdt = xs * dt_e
    alast_e = acs_e[lr - 1:lr, :]
    xdtw = (xdt * jnp.exp(alast_e - acs_e)).astype(BF16)
    ea_e = jnp.exp(acs_e)
    xdt_b = xdt.astype(BF16)

    ii = lax.broadcasted_iota(jnp.int32, (yr, q), 0)
    jj = lax.broadcasted_iota(jnp.int32, (yr, q), 1)
    causal = ii >= jj
    lane = lax.broadcasted_iota(jnp.int32, (q, LANES), 1)
    even_half = lane < SSM_HEAD_DIM
    zero_b = jnp.zeros((q, LANES), BF16)

    ys = []
    for g in range(SSM_GROUPS):
        bg = xbc[:, SSM_D_INNER + g * SSM_D_STATE:SSM_D_INNER + (g + 1) * SSM_D_STATE]
        cg = xbc[:, SSM_D_INNER + SSM_GROUPS * SSM_D_STATE + g * SSM_D_STATE:
                 SSM_D_INNER + SSM_GROUPS * SSM_D_STATE + (g + 1) * SSM_D_STATE].astype(BF16)
        bg_t = bg.T.astype(BF16)
        gcols = slice(g * GROUP_W, (g + 1) * GROUP_W)
        cb = jnp.dot(cg[:yr], bg_t, preferred_element_type=F32)
        h_in = ht_scr[g]
        y_off = jnp.dot(cg[:yr], h_in.astype(BF16), preferred_element_type=F32) * ea_e[:yr, gcols]
        st = jnp.dot(bg_t, xdtw[:, gcols], preferred_element_type=F32)
        ht_scr[g] = h_in * jnp.exp(alast_e[:, gcols]) + st
        pieces = []
        for pr in range(GROUP_W // LANES):
            ms = []
            for hh in range(2):
                h = g * (SSM_HEADS // SSM_GROUPS) + 2 * pr + hh
                seg = a_cs[:yr, h:h + 1] - a_cs_t[h:h + 1, :]
                ms.append((cb * jnp.exp(jnp.where(causal, seg, NEG_INF))).astype(BF16))
            xblk = xdt_b[:, g * GROUP_W + pr * LANES:g * GROUP_W + (pr + 1) * LANES]
            rhs = jnp.concatenate([jnp.where(even_half, xblk, zero_b),
                                   jnp.where(even_half, zero_b, xblk)], axis=0)
            pieces.append(jnp.dot(jnp.concatenate(ms, axis=1), rhs, preferred_element_type=F32))
        ys.append(jnp.concatenate(pieces, axis=1) + y_off)
    y = jnp.concatenate(ys, axis=1) + dexp_ref[...] * xs[:yr]
    z = proj_ref[:, :SSM_D_INNER]
    if yr > lr:
        z = jnp.concatenate([z, jnp.zeros((yr - lr, SSM_D_INNER), F32)], axis=0)
    y = y * _silu(z)
    for g in range(SSM_GROUPS):
        gcols = slice(g * GROUP_W, (g + 1) * GROUP_W)
        yg = y[:, gcols]
        ms = jnp.mean(yg * yg, axis=-1, keepdims=True)
        yn = yg * lax.rsqrt(ms + EPS) * normg_ref[:, gcols]
        mixed_o[:, gcols] = yn[:lr].astype(BF16)

    pos = pos0 + c * q + lax.broadcasted_iota(jnp.int32, (yr, POOL_GROUP), 0)
    for gi, w in enumerate(POOL_WINDOWS):
        cols = slice(gi * POOL_GROUP, (gi + 1) * POOL_GROUP)
        cur = ext_p[pl.ds(POOL_HALO, yr), cols]
        wsum = cur
        for k in range(1, w):
            wsum = wsum + ext_p[pl.ds(POOL_HALO - k, yr), cols]
        cnt = jnp.minimum(w, pos + 1).astype(F32)
        pooled = wsum / cnt - cur
        yp = jnp.dot(pooled.astype(BF16), poolw_ref[gi], preferred_element_type=F32) * pools_ref[:, cols]
        mixed_o[:, SSM_D_INNER + gi * POOL_GROUP:SSM_D_INNER + (gi + 1) * POOL_GROUP] = yp[:lr].astype(BF16)

    new_c = ext_c[lr:lr + CONV_HALO, :]
    new_p = ext_p[lr:lr + POOL_HALO, :]
    ext_c[0:CONV_HALO, :] = new_c
    ext_p[0:POOL_HALO, :] = new_p

    @pl.when(c == nc - 1)
    def _():
        convn_o[0] = new_c
        pooln_o[0] = new_p
        for g in range(SSM_GROUPS):
            ssmn_o[0, g * GROUP_W:(g + 1) * GROUP_W, :] = ht_scr[g].T


def ssd_pool(proj, n, l, conv_prev, ssm_prev, pool_prev, pos0, p):
    q = SSM_CHUNK
    if l % q == 0:
        lr, nc = q, l // q
    elif l < q and l % SUBLANES == 0:
        lr, nc = l, 1
    else:
        raise ValueError(f"unsupported sequence length {l}")
    convp = jnp.pad(conv_prev, ((0, 0), (CONV_HALO - (SSM_CONV - 1), 0), (0, 0)))
    poolp = jnp.pad(pool_prev, ((0, 0), (POOL_HALO - POOL_HIST, 0), (0, 0)))
    ssmp = ssm_prev.reshape(n, SSM_D_INNER, SSM_D_STATE)
    pad_h = LANES - SSM_HEADS
    tril = jnp.asarray(np.tril(np.ones((q, q), np.float32)), BF16)
    rexp_np = np.zeros((LANES, SSM_D_INNER), np.float32)
    rexp_np[np.arange(SSM_D_INNER) // SSM_HEAD_DIM, np.arange(SSM_D_INNER)] = 1.0
    rexp = jnp.asarray(rexp_np, BF16)
    c2 = lambda b, c: (0, 0)
    seq3 = lambda b, c: (b, 0, 0)
    kern = functools.partial(_ssd_pool_kernel, lr=lr, pos0=pos0)
    mixed, convn, ssmn, pooln = pl.pallas_call(
        kern,
        grid=(n, nc),
        in_specs=[pl.BlockSpec((lr, HYB_PROJ), lambda b, c: (b * nc + c, 0)),
                  pl.BlockSpec((1, CONV_HALO, SSM_CONV_DIM), seq3),
                  pl.BlockSpec((1, SSM_D_INNER, SSM_D_STATE), seq3),
                  pl.BlockSpec((1, POOL_HALO, POOL_DIM), seq3),
                  pl.BlockSpec((SSM_CONV, SSM_CONV_DIM), c2),
                  pl.BlockSpec((1, SSM_CONV_DIM), c2),
                  pl.BlockSpec((1, LANES), c2),
                  pl.BlockSpec((1, LANES), c2),
                  pl.BlockSpec((1, SSM_D_INNER), c2),
                  pl.BlockSpec((1, SSM_D_INNER), c2),
                  pl.BlockSpec((len(POOL_WINDOWS), POOL_GROUP, POOL_GROUP), lambda b, c: (0, 0, 0)),
                  pl.BlockSpec((1, POOL_DIM), c2),
                  pl.BlockSpec((q, q), c2),
                  pl.BlockSpec((LANES, SSM_D_INNER), c2)],
        out_specs=[pl.BlockSpec((lr, HYB_MIX), lambda b, c: (b * nc + c, 0)),
                   pl.BlockSpec((1, CONV_HALO, SSM_CONV_DIM), seq3),
                   pl.BlockSpec((1, SSM_D_INNER, SSM_D_STATE), seq3),
                   pl.BlockSpec((1, POOL_HALO, POOL_DIM), seq3)],
        out_shape=[jax.ShapeDtypeStruct((n * l, HYB_MIX), BF16),
                   jax.ShapeDtypeStruct((n, CONV_HALO, SSM_CONV_DIM), F32),
                   jax.ShapeDtypeStruct((n, SSM_D_INNER, SSM_D_STATE), F32),
                   jax.ShapeDtypeStruct((n, POOL_HALO, POOL_DIM), F32)],
        scratch_shapes=[pltpu.VMEM((CONV_HALO + q, SSM_CONV_DIM), F32),
                        pltpu.VMEM((POOL_HALO + q, POOL_DIM), F32),
                        pltpu.VMEM((SSM_GROUPS, SSM_D_STATE, GROUP_W), F32)],
        compiler_params=_params(("arbitrary", "arbitrary")),
        name="ssd_pool",
    )(proj, convp, ssmp, poolp,
      p["ssm_conv_w"], p["ssm_conv_b"].reshape(1, -1),
      jnp.pad(p["ssm_dt_bias"], (0, pad_h)).reshape(1, LANES),
      jnp.pad(p["ssm_a_log"], (0, pad_h)).reshape(1, LANES),
      jnp.repeat(p["ssm_d"], SSM_HEAD_DIM).reshape(1, SSM_D_INNER),
      p["ssm_norm_g"].reshape(1, SSM_D_INNER),
      p["pool_w"].astype(BF16), p["pool_scale"].reshape(1, POOL_DIM), tril, rexp)
    conv_new = convn[:, CONV_HALO - (SSM_CONV - 1):, :]
    ssm_new = ssmn.reshape(n, SSM_HEADS, SSM_HEAD_DIM, SSM_D_STATE)
    pool_new = pooln[:, POOL_HALO - POOL_HIST:, :]
    return mixed, conv_new, ssm_new, pool_new


def _bucket_bounds():
    max_exact = REL_BUCKETS // 2
    bounds = []
    for b in range(max_exact + 1, REL_BUCKETS):
        d = max_exact
        while True:
            val = math.log(d / max_exact) / math.log(REL_MAX_DIST / max_exact) * (REL_BUCKETS - max_exact)
            if max_exact + int(val) >= b:
                break
            d += 1
        bounds.append(d)
    return bounds


def _bias_kernel(rb_ref, o_ref, *, tiles, rows, cols, sign):
    h = pl.program_id(0)
    max_exact = REL_BUCKETS // 2
    bounds = _bucket_bounds()
    far = rb_ref[REL_BUCKETS - 1, h]
    ii = lax.broadcasted_iota(jnp.int32, (rows, cols), 0)
    jj = lax.broadcasted_iota(jnp.int32, (rows, cols), 1)
    for oi, (kind, off) in enumerate(tiles):
        if kind == "zero":
            o_ref[0, oi] = jnp.zeros((rows, cols), F32)
            continue
        if kind == "neg":
            o_ref[0, oi] = jnp.full((rows, cols), NEG_INF, F32)
            continue
        rel = off + sign * (ii - jj)
        dist = jnp.maximum(rel, 0)
        large = jnp.full((rows, cols), max_exact, jnp.int32)
        for bnd in bounds:
            large = large + (dist >= bnd).astype(jnp.int32)
        bucket = jnp.where(dist < max_exact, dist, large)
        bias = jnp.zeros((rows, cols), F32)
        for b in range(REL_BUCKETS):
            bias = bias + jnp.where(bucket == b, (rb_ref[b, h] - far) * LOG2E, 0.0)
        if kind == "bias_causal":
            bias = jnp.where(rel >= 0, bias, NEG_INF)
        o_ref[0, oi] = bias


def bias_tiles(rel_bias, tiles, rows, cols, sign=1):
    kern = functools.partial(_bias_kernel, tiles=tuple(tiles), rows=rows, cols=cols, sign=sign)
    return pl.pallas_call(
        kern,
        grid=(ATT_HEADS,),
        in_specs=[pl.BlockSpec(memory_space=pltpu.SMEM)],
        out_specs=pl.BlockSpec((1, len(tiles), rows, cols), lambda h: (h, 0, 0, 0)),
        out_shape=jax.ShapeDtypeStruct((ATT_HEADS, len(tiles), rows, cols), F32),
        compiler_params=_params(("arbitrary",)),
        name="bias_tiles",
    )(rel_bias)


def _lambda(lam_ref, lam_init):
    l1 = jnp.sum(lam_ref[0:1, :] * lam_ref[1:2, :], axis=-1, keepdims=True)
    l2 = jnp.sum(lam_ref[2:3, :] * lam_ref[3:4, :], axis=-1, keepdims=True)
    return jnp.exp(l1) - jnp.exp(l2) + lam_init


def _subln(o, sg_ref, lam_init):
    ms = jnp.mean(o * o, axis=-1, keepdims=True)
    return o * lax.rsqrt(ms + EPS) * sg_ref[...] * (1.0 - lam_init)


def _attn_kernel(lam_ref, q_ref, k_ref, vt_ref, bias_ref, sgt_ref, o_ref,
                 m_scr, l_scr, acc_scr, s_a, s_b, p_a, p_b, al_a, al_b, *, tq, lam_init):
    qi = pl.program_id(2)
    q = q_ref[...]
    lane = lax.broadcasted_iota(jnp.int32, (tq, LANES), 1)
    zero = jnp.zeros_like(q)
    q_halves = (jnp.where(lane < ATT_DH, q, zero), jnp.where(lane < ATT_DH, zero, q))
    m_scr[...] = jnp.full(m_scr.shape, NEG_INF, F32)
    l_scr[...] = jnp.zeros(l_scr.shape, F32)
    acc_scr[...] = jnp.zeros(acc_scr.shape, F32)
    p_b[...] = jnp.zeros(p_b.shape, BF16)
    al_b[...] = jnp.ones(al_b.shape, F32)

    def block_offset(kb):
        return pl.multiple_of(jnp.clip(kb, 0, qi) * tq, tq)

    def scores(kb, s_ref):
        k = k_ref[pl.ds(block_offset(kb), tq), :]
        for mi in range(2):
            s_ref[mi] = lax.dot_general(k, q_halves[mi], NT_DIMS, preferred_element_type=F32)

    def softmax(kb, s_ref, p_ref, al_ref, far):
        bias_t = None if far else bias_ref[0, jnp.clip(kb - qi + 2, 0, 3)]
        for mi in range(2):
            s = s_ref[mi] if far else s_ref[mi] + bias_t
            m_old = m_scr[mi]
            m_new = jnp.maximum(m_old, jnp.max(s, axis=0, keepdims=True))
            alpha = jnp.exp2(m_old - m_new)
            pr = jnp.exp2(s - m_new)
            l_scr[mi] = alpha * l_scr[mi] + jnp.sum(pr, axis=0, keepdims=True)
            m_scr[mi] = m_new
            p_ref[mi] = pr.astype(BF16)
            al_ref[mi] = alpha

    def values(kb, p_ref, al_ref):
        vt = vt_ref[:, pl.ds(block_offset(kb), tq)]
        for mi in range(2):
            acc_scr[mi] = al_ref[mi] * acc_scr[mi] + jnp.dot(vt, p_ref[mi], preferred_element_type=F32)

    def pair(j, carry, far):
        a = 2 * j
        scores(a + 1, s_b)
        softmax(a, s_a, p_a, al_a, far)
        values(a - 1, p_b, al_b)
        scores(a + 2, s_a)
        softmax(a + 1, s_b, p_b, al_b, far)
        values(a, p_a, al_a)
        return carry

    n_pairs = (qi + 2) // 2
    n_far = jnp.maximum(qi - 1, 0) // 2
    scores(0, s_a)
    lax.fori_loop(0, n_far, functools.partial(pair, far=True), 0)
    lax.fori_loop(n_far, n_pairs, functools.partial(pair, far=False), 0)
    values(2 * n_pairs - 1, p_b, al_b)
    lam = _lambda(lam_ref, lam_init)
    o_t = acc_scr[0] / l_scr[0] - lam * (acc_scr[1] / l_scr[1])
    ms = jnp.mean(o_t * o_t, axis=0, keepdims=True)
    o_t = o_t * lax.rsqrt(ms + EPS) * sgt_ref[...] * (1.0 - lam_init)
    o_ref[...] = o_t.T.astype(BF16)


def diff_attention_prompt(q, k, vt, n, l, bias_t, lam_vecs, subln_g, lam_init, tq):
    nq = l // tq
    kern = functools.partial(_attn_kernel, tq=tq, lam_init=lam_init)
    sg_t = jnp.broadcast_to(subln_g.reshape(ATT_DV, 1), (ATT_DV, tq))
    return pl.pallas_call(
        kern,
        grid=(n, ATT_HEADS, nq),
        in_specs=[pl.BlockSpec((4, ATT_DH), lambda b, h, i: (0, 0)),
                  pl.BlockSpec((tq, LANES), lambda b, h, i: (b * nq + i, h)),
                  pl.BlockSpec((l, LANES), lambda b, h, i: (b, h)),
                  pl.BlockSpec((LANES, l), lambda b, h, i: (h, b)),
                  pl.BlockSpec((1, 4, tq, tq), lambda b, h, i: (h, 0, 0, 0)),
                  pl.BlockSpec((ATT_DV, tq), lambda b, h, i: (0, 0))],
        out_specs=pl.BlockSpec((tq, LANES), lambda b, h, i: (b * nq + i, h)),
        out_shape=jax.ShapeDtypeStruct((n * l, ATT_WIDTH), BF16),
        scratch_shapes=[pltpu.VMEM((2, 1, tq), F32), pltpu.VMEM((2, 1, tq), F32),
                        pltpu.VMEM((2, ATT_DV, tq), F32),
                        pltpu.VMEM((2, tq, tq), F32), pltpu.VMEM((2, tq, tq), F32),
                        pltpu.VMEM((2, tq, tq), BF16), pltpu.VMEM((2, tq, tq), BF16),
                        pltpu.VMEM((2, 1, tq), F32), pltpu.VMEM((2, 1, tq), F32)],
        compiler_params=_params(("arbitrary", "arbitrary", "arbitrary")),
        name="diff_attn_prompt",
    )(lam_vecs, q, k, vt, bias_t, sg_t)


def _attn_decode_kernel(pt_ref, lam_ref, q_ref, kn_ref, vn_ref, pmask_ref, nmask_ref, sg_ref, *rest,
                        pps, l, lam_init):
    k_refs = rest[:pps]
    v_refs = rest[pps:2 * pps]
    o_ref = rest[2 * pps]
    q2_scr, m_scr, l_scr, acc_scr = rest[2 * pps + 1:]
    s_idx = pl.program_id(1)
    ns = pl.num_programs(1)
    last = s_idx == ns - 1

    @pl.when(s_idx == 0)
    def _():
        qf = q_ref[0].astype(F32)
        lane = lax.broadcasted_iota(jnp.int32, (l, LANES), 1)
        pieces = []
        for h in range(ATT_HEADS):
            qh = qf[:, h * LANES:(h + 1) * LANES]
            pieces += [jnp.where(lane < ATT_DH, qh, 0.0), jnp.where(lane < ATT_DH, 0.0, qh)]
        q2_scr[...] = jnp.concatenate(pieces, axis=0).astype(BF16)
        m_scr[...] = jnp.full(m_scr.shape, NEG_INF, F32)
        l_scr[...] = jnp.zeros(l_scr.shape, F32)
        acc_scr[...] = jnp.zeros(acc_scr.shape, F32)

    def update(s, vb):
        m_old = m_scr[...]
        m_new = jnp.maximum(m_old, jnp.max(s, axis=-1, keepdims=True))
        alpha = jnp.exp2(m_old - m_new)
        pr = jnp.exp2(s - m_new)
        l_scr[...] = alpha * l_scr[...] + jnp.sum(pr, axis=-1, keepdims=True)
        acc_scr[...] = alpha * acc_scr[...] + jnp.dot(pr.astype(BF16), vb, preferred_element_type=F32)
        m_scr[...] = m_new

    q2 = q2_scr[...]
    s_parts = []
    for pi in range(pps):
        kb = k_refs[pi][0].astype(BF16)
        s = lax.dot_general(q2, kb, NT_DIMS, preferred_element_type=F32)
        mask = jnp.where(last, pmask_ref[1], pmask_ref[0]) if pi == pps - 1 else pmask_ref[0]
        s_parts.append(s + mask)
    vb = jnp.concatenate([vr[0].astype(BF16) for vr in v_refs], axis=0)
    update(jnp.concatenate(s_parts, axis=1), vb)

    @pl.when(last)
    def _():
        nk = kn_ref.shape[1]
        pad = jnp.zeros((PAGE - nk, LANES), F32)
        kn = jnp.concatenate([kn_ref[0], pad], axis=0).astype(BF16)
        vn = jnp.concatenate([vn_ref[0], pad], axis=0).astype(BF16)
        sn = lax.dot_general(q2, kn, NT_DIMS, preferred_element_type=F32) + nmask_ref[...]
        update(sn, vn)
        lam = _lambda(lam_ref, lam_init)
        outs = []
        for h in range(ATT_HEADS):
            r0 = 2 * h * l
            o0 = acc_scr[r0:r0 + l, :] / l_scr[r0:r0 + l, :]
            o1 = acc_scr[r0 + l:r0 + 2 * l, :] / l_scr[r0 + l:r0 + 2 * l, :]
            outs.append(_subln(o0 - lam * o1, sg_ref, lam_init))
        o_ref[0] = jnp.concatenate(outs, axis=1).astype(BF16)


def diff_attention_decode(q, k_new, v_new, cache_k, cache_v, page_table, pmask, nmask, lam_vecs, subln_g, lam_init):
    n, l, _ = q.shape
    n_pages = page_table.shape[1]
    pps = next(c for c in (DECODE_PAGES_PER_STEP, 4, 2, 1) if n_pages % c == 0)
    ns = n_pages // pps
    rows = 2 * ATT_HEADS * l
    pk = PAGE * ATT_HEADS
    kern = functools.partial(_attn_decode_kernel, pps=pps, l=l, lam_init=lam_init)

    def page_spec(pi):
        return pl.BlockSpec((1, pk, LANES), lambda b, s, pt: (pt[b * n_pages + s * pps + pi], 0, 0))

    seq = lambda b, s, pt: (b, 0, 0)
    grid_spec = pltpu.PrefetchScalarGridSpec(
        num_scalar_prefetch=1,
        grid=(n, ns),
        in_specs=[pl.BlockSpec((4, ATT_DH), lambda b, s, pt: (0, 0)),
                  pl.BlockSpec((1, l, ATT_WIDTH), seq),
                  pl.BlockSpec((1, l * ATT_HEADS, LANES), seq),
                  pl.BlockSpec((1, l * ATT_HEADS, LANES), seq),
                  pl.BlockSpec((2, rows, pk), lambda b, s, pt: (0, 0, 0)),
                  pl.BlockSpec((rows, PAGE), lambda b, s, pt: (0, 0)),
                  pl.BlockSpec((1, ATT_DV), lambda b, s, pt: (0, 0))]
                 + [page_spec(pi) for pi in range(pps)] * 2,
        out_specs=pl.BlockSpec((1, l, ATT_WIDTH), seq),
        scratch_shapes=[pltpu.VMEM((rows, LANES), BF16), pltpu.VMEM((rows, 1), F32),
                        pltpu.VMEM((rows, 1), F32), pltpu.VMEM((rows, ATT_DV), F32)])
    return pl.pallas_call(
        kern,
        grid_spec=grid_spec,
        out_shape=jax.ShapeDtypeStruct((n, l, ATT_WIDTH), BF16),
        compiler_params=_params(("arbitrary", "arbitrary")),
        name="diff_attn_decode",
    )(page_table.reshape(-1), lam_vecs, q, k_new, v_new, pmask, nmask, subln_g.reshape(1, ATT_DV),
      *([cache_k] * pps), *([cache_v] * pps))


def decode_masks(bt, l):
    rows = 2 * ATT_HEADS * l
    r = np.arange(rows)
    row_head, row_tok = r // (2 * l), r % l
    c = np.arange(PAGE * ATT_HEADS)
    head_ok = (c[None, :] % ATT_HEADS) == row_head[:, None]
    neg = np.where(head_ok, 0.0, -np.inf).astype(np.float32)
    cn = np.arange(PAGE)
    key_n = cn // ATT_HEADS
    new_ok = ((cn[None, :] % ATT_HEADS) == row_head[:, None]) & (key_n[None, :] <= row_tok[:, None]) \
        & (key_n[None, :] < l)
    neg_n = np.where(new_ok, 0.0, -np.inf).astype(np.float32)
    per_row = jnp.broadcast_to(bt[:, :, None], (ATT_HEADS, 2, 2, l, PAGE))
    last = jnp.repeat(per_row[:, 0].reshape(rows, PAGE), ATT_HEADS, axis=1)
    new = jnp.repeat(per_row[:, 1].reshape(rows, PAGE)[:, :PAGE // ATT_HEADS], ATT_HEADS, axis=1)
    pmask = jnp.stack([jnp.asarray(neg), last + neg])
    nmask = new + neg_n
    return pmask, nmask


_PEER_CAND = [(i, j) for i in range(PEER_NTOP) for j in range(PEER_NTOP) if (i + 1) * (j + 1) <= PEER_NTOP]


def _oddeven_merge(lo, hi, r):
    step = r * 2
    if step < hi - lo:
        yield from _oddeven_merge(lo, hi, step)
        yield from _oddeven_merge(lo + r, hi, step)
        yield from [(i, i + r) for i in range(lo + r, hi - r, step)]
    else:
        yield (lo, lo + r)


def _oddeven_merge_sort(lo, hi):
    if hi - lo >= 1:
        mid = lo + (hi - lo) // 2
        yield from _oddeven_merge_sort(lo, mid)
        yield from _oddeven_merge_sort(mid + 1, hi)
        yield from _oddeven_merge(lo, hi, 1)


_SORT16 = tuple(_oddeven_merge_sort(0, PEER_NKEYS // SUBLANES - 1))


def _route_kernel(hn_ref, wq_ref, sk_ref, r2_o, e2_o, n1_o, e1_o, q_scr, s_scr, a_scr, b_scr):
    q_scr[...] = jnp.dot(hn_ref[...], wq_ref[...], preferred_element_type=F32)
    groups = PEER_NKEYS // SUBLANES
    for h in range(PEER_HEADS):
        for m in range(2):
            c0 = (2 * h + m) * LANES
            qs = q_scr[:, c0:c0 + LANES].astype(BF16)
            s = lax.dot_general(sk_ref[m], qs, NT_DIMS, preferred_element_type=F32)
            s_scr[m, h] = s
            top_scr = a_scr if m == 0 else b_scr
            col = [s[g * SUBLANES:(g + 1) * SUBLANES, :] for g in range(groups)]
            for (i, j) in _SORT16:
                col[i], col[j] = jnp.maximum(col[i], col[j]), jnp.minimum(col[i], col[j])
            for r in range(PEER_NTOP):
                mx = jnp.max(col[0], axis=0, keepdims=True)
                top_scr[r, h:h + 1, :] = mx
                left = PEER_NTOP - 1 - r
                if left:
                    hit = col[0] == mx
                    depth = min(left, groups)
                    for g in range(depth):
                        below = col[g + 1] if g + 1 < groups else NEG_INF
                        col[g] = jnp.where(hit, below, col[g])
            if m == 1:
                rank = jnp.full(s.shape, float(PEER_NTOP), F32)
                for b in reversed(range(PEER_NTOP)):
                    rank = jnp.where(s >= b_scr[b, h:h + 1, :], float(b), rank)
                r2_o[h] = rank.astype(BF16)
    a_top = [a_scr[r] for r in range(PEER_NTOP)]
    b_top = [b_scr[r] for r in range(PEER_NTOP)]
    cands = [a_top[i] + b_top[j] for (i, j) in _PEER_CAND]
    xs = list(cands)
    ranked = []
    for r in range(PEER_NTOP):
        mx = functools.reduce(jnp.maximum, xs)
        ranked.append(mx)
        if r + 1 < PEER_NTOP:
            xs = [jnp.where(x == mx, NEG_INF, x) for x in xs]
    thr = 0.5 * (ranked[PEER_TOPK - 1] + ranked[PEER_TOPK])
    top = a_top[0] + b_top[0]
    z = functools.reduce(lambda u, w: u + w, [jnp.where(c >= thr, jnp.exp(c - top), 0.0) for c in cands])
    scale = 1.0 / z
    for h in range(PEER_HEADS):
        s1 = s_scr[0, h]
        s2 = s_scr[1, h]
        a0 = a_top[0][h:h + 1]
        b0 = b_top[0][h:h + 1]
        e2_o[h] = jnp.exp(s2 - b0).astype(BF16)
        need = thr[h:h + 1] - s1
        n1 = jnp.zeros(s1.shape, F32)
        for b in range(PEER_NTOP):
            n1 = jnp.where(b_top[b][h:h + 1] >= need, float(b + 1), n1)
        n1_o[h] = n1
        e1_o[h] = jnp.exp(s1 - a0) * scale[h:h + 1]


def peer_route(hn, wq, sk):
    t, d = hn.shape
    tm = min(256, t)
    spec = pl.BlockSpec((PEER_HEADS, PEER_NKEYS, tm), lambda i: (0, 0, i))
    shape = lambda dt: jax.ShapeDtypeStruct((PEER_HEADS, PEER_NKEYS, t), dt)
    return pl.pallas_call(
        _route_kernel,
        grid=(t // tm,),
        in_specs=[pl.BlockSpec((tm, d), lambda i: (i, 0)),
                  pl.BlockSpec(wq.shape, lambda i: (0, 0)),
                  pl.BlockSpec(sk.shape, lambda i: (0, 0, 0))],
        out_specs=[spec, spec, spec, spec],
        out_shape=[shape(BF16), shape(BF16), shape(F32), shape(F32)],
        scratch_shapes=[pltpu.VMEM((tm, wq.shape[1]), F32),
                        pltpu.VMEM((2, PEER_HEADS, PEER_NKEYS, tm), F32),
                        pltpu.VMEM((PEER_NTOP, PEER_HEADS, tm), F32),
                        pltpu.VMEM((PEER_NTOP, PEER_HEADS, tm), F32)],
        compiler_params=_params(("arbitrary",)),
        name="peer_route",
    )(hn, wq, sk)


GELU_C0 = math.sqrt(2.0 / math.pi)
GELU_C1 = GELU_C0 * 0.044715
GELU_K0 = -2.0 * GELU_C0 * LOG2E
GELU_K1 = -2.0 * GELU_C1 * LOG2E


def _peer_kernel(hn_ref, u_ref, vt_ref, r2_ref, e2_ref, n1_ref, e1_ref, x1_ref, g2_ref, o_ref,
                 acc_scr, act_scr, wa_scr):
    e = pl.program_id(2)
    tm = hn_ref.shape[0]

    @pl.when(e == 0)
    def _():
        acc_scr[...] = jnp.zeros(acc_scr.shape, F32)

    act_scr[...] = lax.dot_general(u_ref[...], hn_ref[...], NT_DIMS, preferred_element_type=F32)
    for il in range(PEER_IB):
        rows = slice(il * PEER_NKEYS, (il + 1) * PEER_NKEYS)
        w = None
        for h in range(PEER_HEADS):
            n1 = jnp.broadcast_to(n1_ref[h, il:il + 1, :], (PEER_NKEYS, tm)).astype(BF16)
            e1 = jnp.broadcast_to(e1_ref[h, il:il + 1, :], (PEER_NKEYS, tm)).astype(BF16)
            e2 = e2_ref[h]
            term = jnp.where(r2_ref[h] < n1, e2 * e1, jnp.zeros_like(e2))
            w = term if w is None else w + term
        x = act_scr[rows, :].astype(BF16)
        decay = jnp.exp2(x * (GELU_K0 + GELU_K1 * (x * x)))
        wa_scr[rows, :] = w * (x / (1.0 + decay))
    acc_scr[...] += jnp.dot(vt_ref[...], wa_scr[...], preferred_element_type=F32)

    @pl.when(e == pl.num_programs(2) - 1)
    def _():
        y = acc_scr[...].T
        o_ref[...] = x1_ref[...] + g2_ref[...] * y.reshape(x1_ref.shape)


def peer_mix(hn, u_b, vt_b, r2t, e2t, n1t, e1t, x1, g2):
    n, l, d = x1.shape
    nb, lb = _row_blocks(n, l)
    tm = nb * lb
    lk = l // lb
    ne = PEER_EXPERTS // PEER_EB
    tok = lambda i, k, e: (i * lk + k, 0)
    keys2 = pl.BlockSpec((PEER_HEADS, PEER_NKEYS, tm), lambda i, k, e: (0, 0, i * lk + k))
    keys1 = pl.BlockSpec((PEER_HEADS, PEER_IB, tm), lambda i, k, e: (0, e, i * lk + k))
    return pl.pallas_call(
        _peer_kernel,
        grid=(n // nb, lk, ne),
        in_specs=[pl.BlockSpec((tm, d), tok),
                  pl.BlockSpec((PEER_EB, d), lambda i, k, e: (e, 0)),
                  pl.BlockSpec((d, PEER_EB), lambda i, k, e: (0, e)),
                  keys2, keys2, keys1, keys1,
                  pl.BlockSpec((nb, lb, d), lambda i, k, e: (i, k, 0)),
                  pl.BlockSpec((nb, 1, d), lambda i, k, e: (i, 0, 0))],
        out_specs=pl.BlockSpec((nb, lb, d), lambda i, k, e: (i, k, 0)),
        out_shape=jax.ShapeDtypeStruct((n, l, d), F32),
        scratch_shapes=[pltpu.VMEM((d, tm), F32), pltpu.VMEM((PEER_EB, tm), F32),
                        pltpu.VMEM((PEER_EB, tm), BF16)],
        compiler_params=_params(("arbitrary", "arbitrary", "arbitrary")),
        name="peer_mix",
    )(hn, u_b, vt_b, r2t, e2t, n1t, e1t, x1, g2)


def peer_ffn_residual(hn, x1, g2, wq_b, sk_b, u_b, vt_b):
    r2t, e2t, n1t, e1t = peer_route(hn, wq_b, sk_b)
    return peer_mix(hn, u_b, vt_b, r2t, e2t, n1t, e1t, x1, g2)


def _trunk(x, mods, pos0, conv_prev, ssm_prev, pool_prev, paged_kv, p, w):
    n, l, d = x.shape
    t = n * l
    sh1, sc1, g1, sh2, sc2, g2 = mods[0]
    proj = norm_inproj(x, p["norm_mix_g"][0], sc1, sh1, w["hyb_in"], tn=HYB_PROJ // 3)
    mixed, conv_new, ssm_new, pool_new = ssd_pool(proj, n, l, conv_prev, ssm_prev, pool_prev, pos0, w["ssm"])
    x1, hn = outproj_residual(mixed, w["hyb_out"], x, g1, p["norm_ffn_g"][0], sc2, sh2)
    x2 = peer_ffn_residual(hn, x1, g2, w["peer_wq"][0], w["peer_sk"][0], w["peer_u"][0], w["peer_vt"][0])

    sh1, sc1, g1, sh2, sc2, g2 = mods[1]
    qb, kf, kb, vf, vt = norm_qkv(x2, p["norm_mix_g"][1], sc1, sh1, w["att_in"], p["att_q_g"][0], p["att_k_g"][0])
    lam_init = 0.8 - 0.6 * math.exp(-0.3 * 1)
    lam_vecs = jnp.stack([p["att_lambda_q1"][0], p["att_lambda_k1"][0],
                          p["att_lambda_q2"][0], p["att_lambda_k2"][0]])
    if paged_kv is None:
        tq = min(ATT_TQ, l)
        bias_t = bias_tiles(p["rel_bias"], (("zero", 0), ("bias", tq), ("bias_causal", 0), ("neg", 0)),
                            tq, tq, sign=-1)
        o = diff_attention_prompt(qb, kb, vt, n, l, bias_t, lam_vecs, p["att_subln_g"][0], lam_init, tq)
    else:
        cache_k, cache_v, page_table = paged_kv
        past_len = page_table.shape[1] * PAGE
        bt = bias_tiles(p["rel_bias"], (("bias", pos0 - (past_len - PAGE)), ("bias", pos0 - past_len)),
                        l, PAGE)
        pmask, nmask = decode_masks(bt, l)
        o = diff_attention_decode(qb.reshape(n, l, ATT_WIDTH), kf.reshape(n, l * ATT_HEADS, ATT_DV),
                                  vf.reshape(n, l * ATT_HEADS, ATT_DV), cache_k, cache_v, page_table,
                                  pmask, nmask, lam_vecs, p["att_subln_g"][0], lam_init).reshape(t, ATT_WIDTH)
    x3, hn = outproj_residual(o, w["att_out"], x2, g1, p["norm_ffn_g"][1], sc2, sh2)
    x4 = peer_ffn_residual(hn, x3, g2, w["peer_wq"][1], w["peer_sk"][1], w["peer_u"][1], w["peer_vt"][1])
    k_new = kf.reshape(1, n, l, ATT_HEADS, 2 * ATT_DH)
    v_new = vf.reshape(1, n, l, ATT_HEADS, ATT_DV)
    return x4, conv_new[None], ssm_new[None], pool_new[None], k_new, v_new


def kernel(x_prompt, x_sample, state_conv, state_ssm, state_pool, cache_k, cache_v, page_table,
           c_prompt, c_sample, rel_bias, norm_mix_g, norm_ffn_g, ada_w, ada_b,
           hyb_w_in, ssm_conv_w, ssm_conv_b, ssm_dt_bias, ssm_a_log, ssm_d, ssm_norm_g,
           pool_w, pool_scale, hyb_w_out, att_w_in, att_q_g, att_k_g,
           att_lambda_q1, att_lambda_k1, att_lambda_q2, att_lambda_k2, att_subln_g, att_w_out,
           peer_wq, peer_subkeys, peer_u, peer_v):
    p = {
        "rel_bias": rel_bias, "norm_mix_g": norm_mix_g, "norm_ffn_g": norm_ffn_g,
        "att_q_g": att_q_g, "att_k_g": att_k_g,
        "att_lambda_q1": att_lambda_q1, "att_lambda_k1": att_lambda_k1,
        "att_lambda_q2": att_lambda_q2, "att_lambda_k2": att_lambda_k2, "att_subln_g": att_subln_g,
    }
    w_in = hyb_w_in[0]
    z_w = w_in[:, :SSM_D_INNER]
    xbc_w = w_in[:, SSM_D_INNER:SSM_D_INNER + SSM_CONV_DIM]
    dt_w = w_in[:, SSM_D_INNER + SSM_CONV_DIM:SSM_D_INNER + SSM_CONV_DIM + SSM_HEADS]
    u_w = w_in[:, SSM_D_INNER + SSM_CONV_DIM + SSM_HEADS:]
    hyb_in = jnp.concatenate([z_w, xbc_w, u_w, jnp.pad(dt_w, ((0, 0), (0, LANES - SSM_HEADS)))], axis=1)
    w = {
        "hyb_in": hyb_in.astype(BF16),
        "hyb_out": hyb_w_out[0].astype(BF16),
        "att_in": att_w_in[0].astype(BF16),
        "att_out": att_w_out[0].astype(BF16),
        "peer_wq": peer_wq.astype(BF16),
        "peer_sk": peer_subkeys.astype(BF16),
        "peer_u": peer_u.astype(BF16),
        "peer_vt": jnp.swapaxes(peer_v, 1, 2).astype(BF16),
        "ssm": {"ssm_conv_w": ssm_conv_w[0], "ssm_conv_b": ssm_conv_b[0], "ssm_dt_bias": ssm_dt_bias[0],
                "ssm_a_log": ssm_a_log[0], "ssm_d": ssm_d[0], "ssm_norm_g": ssm_norm_g[0],
                "pool_w": pool_w[0], "pool_scale": pool_scale[0]},
    }
    nb, ns = x_prompt.shape[0], x_sample.shape[0]
    d = x_prompt.shape[2]
    rows = nb + ns
    rpad = -rows % SUBLANES
    c_all = jnp.concatenate([c_prompt, c_sample, jnp.zeros((rpad, d), F32)], axis=0)
    mod = ada_mod(c_all, ada_w, ada_b)

    def mods_for(lo, cnt):
        return [tuple(mod[i, lo:lo + cnt, k * d:(k + 1) * d].reshape(cnt, 1, d) for k in range(6))
                for i in range(mod.shape[0])]

    conv0 = jnp.zeros((nb, SSM_CONV - 1, SSM_CONV_DIM), F32)
    ssm0 = jnp.zeros((nb, SSM_HEADS, SSM_HEAD_DIM, SSM_D_STATE), F32)
    pool0 = jnp.zeros((nb, POOL_HIST, POOL_DIM), F32)
    yp, conv_p, ssm_p, pool_p, k_p, v_p = _trunk(x_prompt, mods_for(0, nb), 0, conv0, ssm0, pool0, None, p, w)
    past_len = page_table.shape[1] * cache_k.shape[2]
    n_phys = cache_k.shape[1]
    ys, conv_s, ssm_s, pool_s, k_s, v_s = _trunk(
        x_sample, mods_for(nb, ns), past_len, state_conv[0], state_ssm[0], state_pool[0],
        (cache_k[0].reshape(n_phys, PAGE * ATT_HEADS, ATT_DV), cache_v[0].reshape(n_phys, PAGE * ATT_HEADS, ATT_DV),
         page_table),
        p, w)
    return (yp, ys, conv_p, ssm_p, pool_p, k_p, v_p, conv_s, ssm_s, pool_s, k_s, v_s)
```
